```python
import jax, jax.numpy as jnp
from jax import lax
import numpy as np

D_MODEL = 2048
BATCH = 1
SEQ = 16384
DEPTH = 2

CHUNK = 64
HEAD_DIM = 128
D_MIX = D_MODEL
D_GDN = D_MIX // 2
D_SB = D_MIX - D_GDN
N_HEADS_GDN = D_GDN // HEAD_DIM
N_HEADS_SB = D_SB // HEAD_DIM
SHORT_CONV = 4
FFN_CONV = 3
D_FF = ((8 * D_MODEL // 3 + 255) // 256) * 256
SB_BLOCK = 128
EPS = 1e-6
D_IN = 4 * D_GDN + 2 * N_HEADS_GDN + 3 * D_SB

kernel_name = "hybrid_gdn_stickbreaking_convffn_adaln"


def rms_norm(x, gain):
    xf = x.astype(jnp.float32)
    y = xf * lax.rsqrt(jnp.mean(xf * xf, axis=-1, keepdims=True) + EPS)
    return (y * gain.astype(jnp.float32)).astype(x.dtype)


def l2_norm(x):
    xf = x.astype(jnp.float32)
    return xf * lax.rsqrt(jnp.sum(xf * xf, axis=-1, keepdims=True) + EPS)


def causal_dwconv(x, w):
    width = w.shape[0]
    t = x.shape[1]
    xp = jnp.pad(x, ((0, 0), (width - 1, 0), (0, 0)))
    y = xp[:, width - 1:width - 1 + t] * w[width - 1]
    for i in range(width - 1):
        y = y + xp[:, i:i + t] * w[i]
    return y


def gated_delta_rule(q, k, v, g, beta):
    b, t, h, dk = q.shape
    dv = v.shape[-1]
    n, c = t // CHUNK, CHUNK

    def chunks(a):
        a = a.reshape((b, n, c, h) + a.shape[3:])
        return jnp.moveaxis(a, (1, 3), (0, 2))

    q, k, v, g, beta = chunks(q), chunks(k), chunks(v), chunks(g), chunks(beta)
    g = jnp.cumsum(g, axis=-1)
    tri_incl = jnp.tril(jnp.ones((c, c), bool))
    tri_strict = jnp.tril(jnp.ones((c, c), bool), -1)
    decay = jnp.exp(jnp.where(tri_incl, g[..., :, None] - g[..., None, :], -jnp.inf))
    k_beta = k * beta[..., None]
    v_beta = v * beta[..., None]
    lower = jnp.where(tri_strict, jnp.einsum('nbhik,nbhjk->nbhij', k_beta, k) * decay, 0.0)
    eye = jnp.eye(c, dtype=q.dtype)
    t_inv = lax.linalg.triangular_solve(eye + lower, jnp.broadcast_to(eye, lower.shape),
                                        left_side=True, lower=True)
    u = t_inv @ v_beta
    w = t_inv @ (k_beta * jnp.exp(g)[..., None])
    attn_intra = jnp.where(tri_incl, jnp.einsum('nbhik,nbhjk->nbhij', q, k) * decay, 0.0)

    def step(state, inp):
        q_c, k_c, u_c, w_c, g_c, a_c = inp
        v_new = u_c - w_c @ state
        o = (q_c * jnp.exp(g_c)[..., None]) @ state + a_c @ v_new
        g_last = g_c[..., -1]
        k_dec = k_c * jnp.exp(g_last[..., None] - g_c)[..., None]
        state = state * jnp.exp(g_last)[..., None, None] + jnp.einsum('bhck,bhcv->bhkv', k_dec, v_new)
        return state, o

    s0 = jnp.zeros((b, h, dk, dv), q.dtype)
    _, o = lax.scan(step, s0, (q, k, u, w, g, attn_intra))
    return jnp.moveaxis(o, (0, 2), (1, 3)).reshape(b, t, h, dv)


def stick_breaking_attention(q, k, v):
    b, t, h, d = q.shape
    nb = t // SB_BLOCK
    kh = jnp.swapaxes(k, 1, 2)
    vh = jnp.swapaxes(v, 1, 2)
    qb = q.reshape(b, nb, SB_BLOCK, h, d).transpose(1, 0, 3, 2, 4)
    key_pos = jnp.arange(t)
    scale = d ** -0.5

    def block(args):
        q_blk, start = args
        q_pos = start + jnp.arange(SB_BLOCK)
        mask = key_pos[None, :] < q_pos[:, None]
        z = jnp.einsum('bhqd,bhkd->bhqk', q_blk, kh) * scale
        log_stay = jnp.where(mask, jax.nn.log_sigmoid(-z), 0.0)
        between = lax.cumsum(log_stay, axis=3, reverse=True) - log_stay
        a = jnp.where(mask, jnp.exp(jax.nn.log_sigmoid(z) + between), 0.0)
        return jnp.einsum('bhqk,bhkd->bhqd', a, vh)

    starts = jnp.arange(nb) * SB_BLOCK
    o = lax.map(block, (qb, starts))
    return o.transpose(1, 0, 3, 2, 4).reshape(b, t, h, d)


def hybrid_layer(x, mod, norm1, w_in, conv_qkv, a_log, dt_bias, gdn_norm,
                 sb_q_norm, sb_k_norm, w_out, norm2, w_up, conv_ffn, w_down):
    b, t, _ = x.shape
    shift1, scale1, gate1, shift2, scale2, gate2 = jnp.split(mod, 6, axis=-1)

    h = rms_norm(x, norm1) * (1.0 + scale1[:, None]) + shift1[:, None]
    proj = h @ w_in
    o0 = 3 * D_GDN
    o1 = o0 + D_GDN
    o2 = o1 + N_HEADS_GDN
    o3 = o2 + N_HEADS_GDN
    qkv_a = jax.nn.silu(causal_dwconv(proj[..., :o0], conv_qkv)).astype(jnp.float32)
    z_a = proj[..., o0:o1].astype(jnp.float32)
    b_a = proj[..., o1:o2].astype(jnp.float32)
    a_a = proj[..., o2:o3].astype(jnp.float32)
    q_b, k_b, v_b = jnp.split(proj[..., o3:].astype(jnp.float32), 3, axis=-1)

    q_a, k_a, v_a = jnp.split(qkv_a, 3, axis=-1)
    q_a = l2_norm(q_a.reshape(b, t, N_HEADS_GDN, HEAD_DIM)) * (HEAD_DIM ** -0.5)
    k_a = l2_norm(k_a.reshape(b, t, N_HEADS_GDN, HEAD_DIM))
    v_a = v_a.reshape(b, t, N_HEADS_GDN, HEAD_DIM)
    beta = jax.nn.sigmoid(b_a)
    g = -jnp.exp(a_log.astype(jnp.float32)) * jax.nn.softplus(a_a + dt_bias.astype(jnp.float32))
    o_a = gated_delta_rule(q_a, k_a, v_a, g, beta)
    o_a = rms_norm(o_a, gdn_norm) * jax.nn.silu(z_a.reshape(b, t, N_HEADS_GDN, HEAD_DIM))
    o_a = o_a.reshape(b, t, D_GDN)

    q_b = rms_norm(q_b.reshape(b, t, N_HEADS_SB, HEAD_DIM), sb_q_norm)
    k_b = rms_norm(k_b.reshape(b, t, N_HEADS_SB, HEAD_DIM), sb_k_norm)
    v_b = v_b.reshape(b, t, N_HEADS_SB, HEAD_DIM)
    o_b = stick_breaking_attention(q_b, k_b, v_b).reshape(b, t, D_SB)

    y = jnp.concatenate([o_a, o_b], axis=-1).astype(x.dtype) @ w_out
    x = x + gate1[:, None] * y

    h2 = rms_norm(x, norm2) * (1.0 + scale2[:, None]) + shift2[:, None]
    gate_br, val_br = jnp.split(h2 @ w_up, 2, axis=-1)
    f = (jax.nn.silu(causal_dwconv(gate_br, conv_ffn)) * val_br) @ w_down
    return x + gate2[:, None] * f


def setup_inputs(seed: int = 0) -> dict:
    key = jax.random.key(seed)
    ks = jax.random.split(key, 20)
    f32 = jnp.float32
    nrm = lambda k, shape, s: jax.random.normal(k, shape, f32) * s
    dt = jnp.exp(jax.random.uniform(ks[8], (DEPTH, N_HEADS_GDN), f32, np.log(1e-3), np.log(1e-1)))
    return {
        "x": nrm(ks[0], (BATCH, SEQ, D_MODEL), 1.0),
        "c": nrm(ks[1], (BATCH, D_MODEL), 1.0),
        "w_ada": nrm(ks[2], (DEPTH, D_MODEL, 6 * D_MODEL), 0.5 * D_MODEL ** -0.5),
        "b_ada": nrm(ks[3], (DEPTH, 6 * D_MODEL), 0.01),
        "norm1": 1.0 + nrm(ks[4], (DEPTH, D_MODEL), 0.01),
        "w_in": nrm(ks[5], (DEPTH, D_MODEL, D_IN), D_MODEL ** -0.5),
        "conv_qkv": nrm(ks[6], (DEPTH, SHORT_CONV, 3 * D_GDN), SHORT_CONV ** -0.5),
        "a_log": jnp.log(jax.random.uniform(ks[7], (DEPTH, N_HEADS_GDN), f32, 1.0, 16.0)),
        "dt_bias": dt + jnp.log(-jnp.expm1(-dt)),
        "gdn_norm": 1.0 + nrm(ks[9], (DEPTH, HEAD_DIM), 0.01),
        "sb_q_norm": 1.0 + nrm(ks[10], (DEPTH, HEAD_DIM), 0.01),
        "sb_k_norm": 1.0 + nrm(ks[11], (DEPTH, HEAD_DIM), 0.01),
        "w_out": nrm(ks[12], (DEPTH, D_MIX, D_MODEL), D_MIX ** -0.5),
        "norm2": 1.0 + nrm(ks[13], (DEPTH, D_MODEL), 0.01),
        "w_up": nrm(ks[14], (DEPTH, D_MODEL, 2 * D_FF), D_MODEL ** -0.5),
        "conv_ffn": nrm(ks[15], (DEPTH, FFN_CONV, D_FF), FFN_CONV ** -0.5),
        "w_down": nrm(ks[16], (DEPTH, D_FF, D_MODEL), D_FF ** -0.5),
    }


def reference(x, c, w_ada, b_ada, norm1, w_in, conv_qkv, a_log, dt_bias, gdn_norm,
              sb_q_norm, sb_k_norm, w_out, norm2, w_up, conv_ffn, w_down):
    c_act = jax.nn.silu(c)
    for l in range(DEPTH):
        mod = c_act @ w_ada[l] + b_ada[l]
        x = hybrid_layer(x, mod, norm1[l], w_in[l], conv_qkv[l], a_log[l], dt_bias[l],
                         gdn_norm[l], sb_q_norm[l], sb_k_norm[l], w_out[l], norm2[l],
                         w_up[l], conv_ffn[l], w_down[l])
    return x
```

```python
import functools

import jax
import jax.numpy as jnp
from jax import lax
from jax.experimental import pallas as pl
from jax.experimental.pallas import tpu as pltpu

F32 = jnp.float32
BF16 = jnp.bfloat16

HEAD_DIM = 128
EPS = 1e-6
SHORT_CONV = 4
FFN_CONV = 3
LANES = 128
GDN_CHUNK = 128
GDN_BLOCK = 256
CONV_HALO = 8
FFN_HALO = 16
SB_BLOCK = 128
F32_EXP_ZERO_BELOW = -104.0
VMEM_LIMIT = 56 * 1024 * 1024


def _params(sem):
    return pltpu.CompilerParams(dimension_semantics=sem, vmem_limit_bytes=VMEM_LIMIT)


def _sigmoid(x):
    return 1.0 / (1.0 + jnp.exp(-x))


def _silu(x):
    return x * _sigmoid(x)


def _dot(a, b):
    return jnp.dot(a, b, preferred_element_type=F32)


def _dot_nt(a, b):
    return lax.dot_general(a, b, (((1,), (1,)), ((), ())), preferred_element_type=F32)


def _split16(a):
    hi = a.astype(BF16)
    return hi, (a - hi.astype(F32)).astype(BF16)


def _dot_split(a, b):
    ah, al = _split16(a)
    bh, bl = _split16(b)
    return _dot(ah, bh) + (_dot(ah, bl) + _dot(al, bh))


def _dot_tn(a, b):
    return lax.dot_general(a, b, (((0,), (0,)), ((), ())), preferred_element_type=F32)


def _mod_kernel(c_ref, w_ref, b_ref, o_ref):
    c = c_ref[...]
    ca = _silu(c)
    o_ref[0] = jnp.sum(ca * w_ref[0], axis=0, keepdims=True) + b_ref[0]


def _modulation(c, w_ada, b_ada):
    depth, d, n = w_ada.shape
    tn = 1024
    return pl.pallas_call(
        _mod_kernel,
        grid=(depth, n // tn),
        in_specs=[
            pl.BlockSpec((d, 1), lambda l, j: (0, 0)),
            pl.BlockSpec((1, d, tn), lambda l, j: (l, 0, j)),
            pl.BlockSpec((1, 1, tn), lambda l, j: (l, 0, j)),
        ],
        out_specs=pl.BlockSpec((1, 1, tn), lambda l, j: (l, 0, j)),
        out_shape=jax.ShapeDtypeStruct((depth, 1, n), F32),
        compiler_params=_params(("arbitrary", "arbitrary")),
        name="adaln_mod",
    )(c.reshape(d, 1), w_ada, b_ada.reshape(depth, 1, n))


def _inproj_kernel(x_ref, g_ref, sc_ref, sh_ref, w_ref, wg_ref, qn_ref, kn_ref,
                   oa_ref, ob_ref, og_ref, h_ref, *, n_a, rows):
    n = pl.program_id(1)
    tm = x_ref.shape[0]

    @pl.when(n == 0)
    def _():
        gain = g_ref[...] * (1.0 + sc_ref[...])
        shift = sh_ref[...]

        def body(r, carry):
            sl = pl.ds(pl.multiple_of(r * rows, rows), rows)
            xf = x_ref[sl, :]
            ms = jnp.mean(xf * xf, axis=-1, keepdims=True)
            h = xf * lax.rsqrt(ms + EPS) * gain + shift
            h_ref[sl, :] = h.astype(BF16)
            return carry

        lax.fori_loop(0, tm // rows, body, 0)
        og_ref[...] = _dot(h_ref[...], wg_ref[...])

    acc = _dot(h_ref[...], w_ref[...])

    @pl.when(n < n_a)
    def _():
        oa_ref[...] = acc

    def headnorm(gain_ref, mult):
        gain = gain_ref[...] * mult
        for h in range(acc.shape[1] // HEAD_DIM):
            a = acc[:, h * HEAD_DIM:(h + 1) * HEAD_DIM]
            ms = jnp.mean(a * a, axis=-1, keepdims=True)
            ob_ref[:, h * HEAD_DIM:(h + 1) * HEAD_DIM] = (a * lax.rsqrt(ms + EPS) * gain).astype(BF16)

    @pl.when(n == n_a)
    def _():
        headnorm(qn_ref, HEAD_DIM ** -0.5)

    @pl.when(n == n_a + 1)
    def _():
        headnorm(kn_ref, 1.0)

    @pl.when(n == n_a + 2)
    def _():
        ob_ref[...] = acc.astype(BF16)


def _inproj(x, gain, scale, shift, w_main, w_gate, qn, kn, *, tm=512):
    t, d = x.shape
    tn = 1024
    n_a = 4
    n_tiles = w_main.shape[1] // tn
    row = lambda m, n: (0, 0)
    return pl.pallas_call(
        functools.partial(_inproj_kernel, n_a=n_a, rows=128),
        grid=(t // tm, n_tiles),
        in_specs=[
            pl.BlockSpec((tm, d), lambda m, n: (m, 0)),
            pl.BlockSpec((1, d), row),
            pl.BlockSpec((1, d), row),
            pl.BlockSpec((1, d), row),
            pl.BlockSpec((d, tn), lambda m, n: (0, n)),
            pl.BlockSpec((d, LANES), row),
            pl.BlockSpec((1, HEAD_DIM), row),
            pl.BlockSpec((1, HEAD_DIM), row),
        ],
        out_specs=[
            pl.BlockSpec((tm, tn), lambda m, n: (m, jnp.minimum(n, n_a - 1))),
            pl.BlockSpec((tm, tn), lambda m, n: (m, jnp.maximum(n - n_a, 0))),
            pl.BlockSpec((tm, LANES), lambda m, n: (m, 0)),
        ],
        out_shape=[
            jax.ShapeDtypeStruct((t, n_a * tn), F32),
            jax.ShapeDtypeStruct((t, (n_tiles - n_a) * tn), BF16),
            jax.ShapeDtypeStruct((t, LANES), F32),
        ],
        scratch_shapes=[pltpu.VMEM((tm, d), BF16)],
        compiler_params=_params(("arbitrary", "arbitrary")),
        name="inproj",
    )(x, gain, scale, shift, w_main, w_gate, qn, kn)


def _gdn_kernel(q_ref, k_ref, v_ref, z_ref, g_ref, cw_ref, alog_ref, dtb_ref, gn_ref, lvl_ref,
                o_ref, s_ref, halo_ref, *, n_heads):
    i = pl.program_id(0)
    tb = q_ref.shape[0]
    c = GDN_CHUNK
    hd = HEAD_DIM
    width = n_heads * hd

    @pl.when(i == 0)
    def _():
        s_ref[...] = jnp.zeros_like(s_ref)
        halo_ref[...] = jnp.zeros_like(halo_ref)

    def conv(ref, idx):
        cur = ref[...]
        xp = jnp.concatenate([halo_ref[idx], cur], axis=0)
        w = cw_ref[:, idx * width:(idx + 1) * width]
        y = xp * w[SHORT_CONV - 1:SHORT_CONV, :]
        for s in range(1, SHORT_CONV):
            y = y + pltpu.roll(xp, s, axis=0) * w[SHORT_CONV - 1 - s:SHORT_CONV - s, :]
        halo_ref[idx] = cur[tb - CONV_HALO:, :]
        return _silu(y[CONV_HALO:, :])

    qc = conv(q_ref, 0)
    kc = conv(k_ref, 1)
    vc = conv(v_ref, 2)

    gl = g_ref[...]
    beta_all = _sigmoid(gl)
    xg = gl + dtb_ref[...]
    softplus = jnp.maximum(xg, 0.0) + jnp.log(1.0 + jnp.exp(-jnp.abs(xg)))
    g_all = -jnp.exp(alog_ref[...]) * softplus

    ii = lax.broadcasted_iota(jnp.int32, (c, c), 0)
    jj = lax.broadcasted_iota(jnp.int32, (c, c), 1)
    incl = ii >= jj
    strict = ii > jj
    tri = jnp.where(incl, 1.0, 0.0).astype(F32)
    eye = jnp.where(ii == jj, 1.0, 0.0).astype(F32)
    lane = lax.broadcasted_iota(jnp.int32, (c, LANES), 1)
    gn = gn_ref[...]

    for ch in range(tb // c):
        r0 = ch * c
        gc_all = jnp.dot(tri, g_all[r0:r0 + c, :], precision=lax.Precision.HIGHEST,
                         preferred_element_type=F32)
        for h in range(n_heads):
            cs = slice(h * hd, (h + 1) * hd)
            q = qc[r0:r0 + c, cs]
            k = kc[r0:r0 + c, cs]
            v = vc[r0:r0 + c, cs]
            q = q * lax.rsqrt(jnp.sum(q * q, axis=-1, keepdims=True) + EPS) * (hd ** -0.5)
            k = k * lax.rsqrt(jnp.sum(k * k, axis=-1, keepdims=True) + EPS)
            gcol = gc_all[:, n_heads + h:n_heads + h + 1]
            bcol = beta_all[r0:r0 + c, h:h + 1]
            sel = jnp.where(lane == n_heads + h, 1.0, 0.0).astype(F32)
            grow = lax.dot_general(sel, gc_all, (((1,), (1,)), ((), ())),
                                   precision=lax.Precision.HIGHEST,
                                   preferred_element_type=F32)
            decay = jnp.exp(jnp.where(incl, gcol - grow, -jnp.inf))
            glast = grow[:, c - 1:c]
            eg = jnp.exp(gcol)
            kb = k * bcol
            vb = v * bcol
            k16 = k.astype(BF16)
            lmat = jnp.where(strict, _dot_nt(kb.astype(BF16), k16) * decay, 0.0)
            amat = jnp.where(incl, _dot_nt(q.astype(BF16), k16) * decay, 0.0)

            tinv = eye - lmat * lvl_ref[0]
            for lvl in range(1, lvl_ref.shape[0]):
                tinv = tinv - _dot_split(tinv, _dot_split(lmat * lvl_ref[lvl], tinv))
            t16 = tinv.astype(BF16)
            u = _dot(t16, vb.astype(BF16))
            w = _dot(t16, (kb * eg).astype(BF16))

            s = s_ref[h]
            s16 = s.astype(BF16)
            v_new = u - _dot(w.astype(BF16), s16)
            vn16 = v_new.astype(BF16)
            o = _dot((q * eg).astype(BF16), s16) + _dot(amat.astype(BF16), vn16)
            kdec = k * jnp.exp(glast - gcol)
            s_ref[h] = s * jnp.exp(glast[0:1, :]) + _dot_tn(kdec.astype(BF16), vn16)

            ms = jnp.mean(o * o, axis=-1, keepdims=True)
            zg = z_ref[r0:r0 + c, cs]
            o_ref[r0:r0 + c, cs] = (o * lax.rsqrt(ms + EPS) * gn * _silu(zg)).astype(BF16)


def _level_masks(c):
    i = lax.broadcasted_iota(jnp.int32, (c, c), 0)
    j = lax.broadcasted_iota(jnp.int32, (c, c), 1)
    levels = []
    for l in range(c.bit_length() - 1):
        same_big = (i >> (l + 1)) == (j >> (l + 1))
        diff_small = (i >> l) != (j >> l)
        levels.append(jnp.where(same_big & diff_small & (i > j), 1.0, 0.0))
    return jnp.stack(levels).astype(F32)


def _gdn(proj_a, gates, conv_w, alog_pad, dtb_pad, gdn_norm, *, n_heads):
    t = proj_a.shape[0]
    lvl = _level_masks(GDN_CHUNK)
    width = n_heads * HEAD_DIM
    tb = GDN_BLOCK
    row = lambda i: (0, 0)
    return pl.pallas_call(
        functools.partial(_gdn_kernel, n_heads=n_heads),
        grid=(t // tb,),
        in_specs=[
            pl.BlockSpec((tb, width), lambda i: (i, 0)),
            pl.BlockSpec((tb, width), lambda i: (i, 1)),
            pl.BlockSpec((tb, width), lambda i: (i, 2)),
            pl.BlockSpec((tb, width), lambda i: (i, 3)),
            pl.BlockSpec((tb, LANES), lambda i: (i, 0)),
            pl.BlockSpec((SHORT_CONV, 3 * width), row),
            pl.BlockSpec((1, LANES), row),
            pl.BlockSpec((1, LANES), row),
            pl.BlockSpec((1, HEAD_DIM), row),
            pl.BlockSpec(lvl.shape, lambda i: (0, 0, 0)),
        ],
        out_specs=pl.BlockSpec((tb, width), lambda i: (i, 0)),
        out_shape=jax.ShapeDtypeStruct((t, width), BF16),
        scratch_shapes=[
            pltpu.VMEM((n_heads, HEAD_DIM, HEAD_DIM), F32),
            pltpu.VMEM((3, CONV_HALO, width), F32),
        ],
        compiler_params=_params(("arbitrary",)),
        name="gdn",
    )(proj_a, proj_a, proj_a, proj_a, gates, conv_w, alog_pad, dtb_pad, gdn_norm, lvl)


def _sb_kernel(zb_ref, q_ref, k_ref, v_ref, u_ref, o_ref, acc_ref, run_ref):
    i = pl.program_id(1)
    b = SB_BLOCK
    q = q_ref[...]
    umat = u_ref[...]
    zb = zb_ref[0]

    def block(j, diagonal):
        start = pl.multiple_of(j * b, b)
        kb = k_ref[pl.ds(start, b), :]
        vb = v_ref[pl.ds(start, b), :]
        z = _dot_nt(q, kb)
        ls = -(jnp.maximum(z, 0.0) + jnp.log(1.0 + jnp.exp(-jnp.abs(z))))
        if diagonal:
            tt = lax.broadcasted_iota(jnp.int32, (b, b), 0)
            ss = lax.broadcasted_iota(jnp.int32, (b, b), 1)
            mask = ss < tt
            ls = jnp.where(mask, ls, 0.0)
        hi = ls.astype(BF16)
        lo = (ls - hi.astype(F32)).astype(BF16)
        cum = _dot(hi, umat) + _dot(lo, umat)
        run = run_ref[...]
        a = jnp.exp(z + run + cum[:, :b])
        if diagonal:
            a = jnp.where(mask, a, 0.0)
        acc_ref[...] += _dot(a.astype(BF16), vb)
        new_run = run + cum[:, b:]
        run_ref[...] = new_run
        return jnp.max(new_run)

    acc_ref[...] = jnp.zeros_like(acc_ref)
    run_ref[...] = jnp.zeros_like(run_ref)
    m0 = block(i, True)

    def cond(carry):
        j, m = carry
        return jnp.logical_and(j >= 0, m + zb > F32_EXP_ZERO_BELOW)

    def body(carry):
        j, _ = carry
        return j - 1, block(j, False)

    lax.while_loop(cond, body, (i - 1, m0))
    o_ref[...] = acc_ref[...].astype(BF16)


def _stick_breaking(proj_b, zbound, umat, *, n_heads):
    t = proj_b.shape[0]
    b = SB_BLOCK
    return pl.pallas_call(
        _sb_kernel,
        grid=(n_heads, t // b),
        in_specs=[
            pl.BlockSpec(memory_space=pltpu.SMEM),
            pl.BlockSpec((b, HEAD_DIM), lambda h, i: (i, h)),
            pl.BlockSpec((t, HEAD_DIM), lambda h, i: (0, n_heads + h)),
            pl.BlockSpec((t, HEAD_DIM), lambda h, i: (0, 2 * n_heads + h)),
            pl.BlockSpec((b, 2 * b), lambda h, i: (0, 0)),
        ],
        out_specs=pl.BlockSpec((b, HEAD_DIM), lambda h, i: (i, h)),
        out_shape=jax.ShapeDtypeStruct((t, n_heads * HEAD_DIM), BF16),
        scratch_shapes=[pltpu.VMEM((b, HEAD_DIM), F32), pltpu.VMEM((b, b), F32)],
        compiler_params=_params(("arbitrary", "arbitrary")),
        name="stick_breaking",
    )(zbound, proj_b, proj_b, proj_b, umat)


def _outproj_kernel(x_ref, oa_ref, ob_ref, wa_ref, wb_ref, gate_ref, o_ref):
    y = _dot(oa_ref[...], wa_ref[...]) + _dot(ob_ref[...], wb_ref[...])
    o_ref[...] = x_ref[...] + gate_ref[...] * y


def _outproj(x, o_a, o_b, w_out16, gate, *, tm=512):
    t, d = x.shape
    da = o_a.shape[1]
    db = o_b.shape[1]
    return pl.pallas_call(
        _outproj_kernel,
        grid=(t // tm,),
        in_specs=[
            pl.BlockSpec((tm, d), lambda m: (m, 0)),
            pl.BlockSpec((tm, da), lambda m: (m, 0)),
            pl.BlockSpec((tm, db), lambda m: (m, 0)),
            pl.BlockSpec((da, d), lambda m: (0, 0)),
            pl.BlockSpec((db, d), lambda m: (1, 0)),
            pl.BlockSpec((1, d), lambda m: (0, 0)),
        ],
        out_specs=pl.BlockSpec((tm, d), lambda m: (m, 0)),
        out_shape=jax.ShapeDtypeStruct((t, d), F32),
        compiler_params=_params(("arbitrary",)),
        name="outproj",
    )(x, o_a, o_b, w_out16, w_out16, gate)


def _ffn_kernel(x_ref, halo_ref, g_ref, sc_ref, sh_ref, gate_ref, wg_ref, wv_ref, cw_ref, wd_ref,
                o_ref, h_ref, acc_ref, *, rows):
    m = pl.program_id(0)
    f = pl.program_id(1)
    nf = pl.num_programs(1)
    tm = x_ref.shape[0]

    @pl.when(f == 0)
    def _():
        gain = g_ref[...] * (1.0 + sc_ref[...])
        shift = sh_ref[...]

        def norm(xf):
            ms = jnp.mean(xf * xf, axis=-1, keepdims=True)
            return (xf * lax.rsqrt(ms + EPS) * gain + shift).astype(BF16)

        keep = jnp.where(m > 0, 1.0, 0.0).astype(F32)
        h_ref[0:FFN_HALO, :] = (norm(halo_ref[...]).astype(F32) * keep).astype(BF16)

        def body(r, carry):
            src = pl.ds(pl.multiple_of(r * rows, rows), rows)
            dst = pl.ds(pl.multiple_of(FFN_HALO + r * rows, FFN_HALO), rows)
            h_ref[dst, :] = norm(x_ref[src, :])
            return carry

        lax.fori_loop(0, tm // rows, body, 0)

    ug = _dot(h_ref[...], wg_ref[...])
    cw = cw_ref[...]
    y = ug * cw[FFN_CONV - 1:FFN_CONV, :]
    for s in range(1, FFN_CONV):
        y = y + pltpu.roll(ug, s, axis=0) * cw[FFN_CONV - 1 - s:FFN_CONV - s, :]
    uv = _dot(h_ref[FFN_HALO:, :], wv_ref[...])
    act = (_silu(y[FFN_HALO:, :]) * uv).astype(BF16)
    part = _dot(act, wd_ref[...])

    @pl.when(f == 0)
    def _():
        acc_ref[...] = part

    @pl.when(f > 0)
    def _():
        acc_ref[...] += part

    @pl.when(f == nf - 1)
    def _():
        o_ref[...] = x_ref[...] + gate_ref[...] * acc_ref[...]


def _ffn(x1, gain, scale, shift, gate, w_up16, conv_w, w_down16, *, tm=512, tf=512):
    t, d = x1.shape
    d_ff = w_down16.shape[0]
    nf = d_ff // tf
    row = lambda m, f: (0, 0)
    halo_blocks = tm // FFN_HALO
    return pl.pallas_call(
        functools.partial(_ffn_kernel, rows=128),
        grid=(t // tm, nf),
        in_specs=[
            pl.BlockSpec((tm, d), lambda m, f: (m, 0)),
            pl.BlockSpec((FFN_HALO, d), lambda m, f: (jnp.maximum(m * halo_blocks - 1, 0), 0)),
            pl.BlockSpec((1, d), row),
            pl.BlockSpec((1, d), row),
            pl.BlockSpec((1, d), row),
            pl.BlockSpec((1, d), row),
            pl.BlockSpec((d, tf), lambda m, f: (0, f)),
            pl.BlockSpec((d, tf), lambda m, f: (0, nf + f)),
            pl.BlockSpec((FFN_CONV, tf), lambda m, f: (0, f)),
            pl.BlockSpec((tf, d), lambda m, f: (f, 0)),
        ],
        out_specs=pl.BlockSpec((tm, d), lambda m, f: (m, 0)),
        out_shape=jax.ShapeDtypeStruct((t, d), F32),
        scratch_shapes=[pltpu.VMEM((tm + FFN_HALO, d), BF16), pltpu.VMEM((tm, d), F32)],
        compiler_params=_params(("arbitrary", "arbitrary")),
        name="convffn",
    )(x1, x1, gain, scale, shift, gate, w_up16, w_up16, conv_w, w_down16)


def kernel(x, c, w_ada, b_ada, norm1, w_in, conv_qkv, a_log, dt_bias, gdn_norm, sb_q_norm,
           sb_k_norm, w_out, norm2, w_up, conv_ffn, w_down):
    batch, t, d = x.shape
    assert batch == 1
    depth = w_ada.shape[0]
    n_heads = a_log.shape[1]
    d_gdn = n_heads * HEAD_DIM
    d_sb = (w_in.shape[2] - 4 * d_gdn - 2 * n_heads) // 3
    assert d_sb == d_gdn and 2 * n_heads <= LANES
    o_gate = 4 * d_gdn
    o_sb = o_gate + 2 * n_heads

    mod = _modulation(c, w_ada, b_ada)
    b = SB_BLOCK
    jr = lax.broadcasted_iota(jnp.int32, (b, 2 * b), 0)
    sc = lax.broadcasted_iota(jnp.int32, (b, 2 * b), 1)
    umat = jnp.where((jr >= sc) | (sc >= b), 1.0, 0.0).astype(BF16)

    xs = x.reshape(t, d)
    for l in range(depth):
        shift1, scale1, gate1, shift2, scale2, gate2 = [mod[l, :, j * d:(j + 1) * d] for j in range(6)]
        w_main = jnp.concatenate([w_in[l, :, :o_gate], w_in[l, :, o_sb:]], axis=1).astype(BF16)
        w_gate = jnp.pad(w_in[l, :, o_gate:o_sb], ((0, 0), (0, LANES - 2 * n_heads))).astype(BF16)
        proj_a, proj_b, gates = _inproj(
            xs, norm1[l].reshape(1, d), scale1, shift1, w_main, w_gate,
            sb_q_norm[l].reshape(1, HEAD_DIM), sb_k_norm[l].reshape(1, HEAD_DIM))

        pad = (n_heads, LANES - 2 * n_heads)
        alog_pad = jnp.pad(a_log[l], pad).reshape(1, LANES)
        dtb_pad = jnp.pad(dt_bias[l], pad).reshape(1, LANES)
        o_a = _gdn(proj_a, gates, conv_qkv[l], alog_pad, dtb_pad,
                   gdn_norm[l].reshape(1, HEAD_DIM), n_heads=n_heads)

        zbound = (1.02 * HEAD_DIM ** 0.5 * jnp.max(jnp.abs(sb_q_norm[l]))
                  * jnp.max(jnp.abs(sb_k_norm[l]))).reshape(1)
        o_b = _stick_breaking(proj_b, zbound, umat, n_heads=n_heads)

        x1 = _outproj(xs, o_a, o_b, w_out[l].astype(BF16), gate1)
        xs = _ffn(x1, norm2[l].reshape(1, d), scale2, shift2, gate2,
                  w_up[l].astype(BF16), conv_ffn[l], w_down[l].astype(BF16))
    return xs.reshape(batch, t, d)
```

```python
import functools

import jax
import jax.numpy as jnp
from jax import lax
from jax.experimental import pallas as pl
from jax.experimental.pallas import tpu as pltpu

F32 = jnp.float32
BF16 = jnp.bfloat16

HEAD_DIM = 128
EPS = 1e-6
SHORT_CONV = 4
FFN_CONV = 3
LANES = 128
GDN_CHUNK = 128
GDN_BLOCK = 256
CONV_HALO = 8
FFN_HALO = 16
SB_QUERY = 64
SB_WINDOW = 256
SB_STEP = 128
SB_GROUP = 8
F32_EXP_ZERO_BELOW = -104.0
VMEM_LIMIT = 56 * 1024 * 1024


def _params(sem):
    return pltpu.CompilerParams(dimension_semantics=sem, vmem_limit_bytes=VMEM_LIMIT)


def _sigmoid(x):
    return 1.0 / (1.0 + jnp.exp(-x))


def _silu(x):
    return x * _sigmoid(x)


def _dot(a, b):
    return jnp.dot(a, b, preferred_element_type=F32)


def _dot_nt(a, b):
    return lax.dot_general(a, b, (((1,), (1,)), ((), ())), preferred_element_type=F32)


def _split16(a):
    hi = a.astype(BF16)
    return hi, (a - hi.astype(F32)).astype(BF16)


def _dot_tn(a, b):
    return lax.dot_general(a, b, (((0,), (0,)), ((), ())), preferred_element_type=F32)


def _mod_kernel(c_ref, w_ref, b_ref, o_ref):
    c = c_ref[...]
    ca = _silu(c)
    o_ref[0] = jnp.sum(ca * w_ref[0], axis=0, keepdims=True) + b_ref[0]


def _modulation(c, w_ada, b_ada):
    depth, d, n = w_ada.shape
    tn = 1024
    return pl.pallas_call(
        _mod_kernel,
        grid=(depth, n // tn),
        in_specs=[
            pl.BlockSpec((d, 1), lambda l, j: (0, 0)),
            pl.BlockSpec((1, d, tn), lambda l, j: (l, 0, j)),
            pl.BlockSpec((1, 1, tn), lambda l, j: (l, 0, j)),
        ],
        out_specs=pl.BlockSpec((1, 1, tn), lambda l, j: (l, 0, j)),
        out_shape=jax.ShapeDtypeStruct((depth, 1, n), F32),
        compiler_params=_params(("arbitrary", "arbitrary")),
        name="adaln_mod",
    )(c.reshape(d, 1), w_ada, b_ada.reshape(depth, 1, n))


def _inproj_kernel(x_ref, g_ref, sc_ref, sh_ref, w_ref, wg_ref, qn_ref, kn_ref,
                   oa_ref, ob_ref, og_ref, h_ref, *, n_a, rows):
    n = pl.program_id(1)
    tm = x_ref.shape[0]

    @pl.when(n == 0)
    def _():
        gain = g_ref[...] * (1.0 + sc_ref[...])
        shift = sh_ref[...]

        def body(r, carry):
            sl = pl.ds(pl.multiple_of(r * rows, rows), rows)
            xf = x_ref[sl, :]
            ms = jnp.mean(xf * xf, axis=-1, keepdims=True)
            h = xf * lax.rsqrt(ms + EPS) * gain + shift
            h_ref[sl, :] = h.astype(BF16)
            return carry

        lax.fori_loop(0, tm // rows, body, 0)
        og_ref[...] = _dot(h_ref[...], wg_ref[...])

    acc = _dot(h_ref[...], w_ref[...])

    @pl.when(n < n_a)
    def _():
        oa_ref[...] = acc

    def headnorm(gain_ref, mult):
        gain = gain_ref[...] * mult
        for h in range(acc.shape[1] // HEAD_DIM):
            a = acc[:, h * HEAD_DIM:(h + 1) * HEAD_DIM]
            ms = jnp.mean(a * a, axis=-1, keepdims=True)
            ob_ref[:, h * HEAD_DIM:(h + 1) * HEAD_DIM] = (a * lax.rsqrt(ms + EPS) * gain).astype(BF16)

    @pl.when(n == n_a)
    def _():
        headnorm(qn_ref, HEAD_DIM ** -0.5)

    @pl.when(n == n_a + 1)
    def _():
        headnorm(kn_ref, 1.0)

    @pl.when(n == n_a + 2)
    def _():
        ob_ref[...] = acc.astype(BF16)


def _inproj(x, gain, scale, shift, w_main, w_gate, qn, kn, *, tm=512):
    t, d = x.shape
    tn = 1024
    n_a = 4
    n_tiles = w_main.shape[1] // tn
    row = lambda m, n: (0, 0)
    return pl.pallas_call(
        functools.partial(_inproj_kernel, n_a=n_a, rows=128),
        grid=(t // tm, n_tiles),
        in_specs=[
            pl.BlockSpec((tm, d), lambda m, n: (m, 0)),
            pl.BlockSpec((1, d), row),
            pl.BlockSpec((1, d), row),
            pl.BlockSpec((1, d), row),
            pl.BlockSpec((d, tn), lambda m, n: (0, n)),
            pl.BlockSpec((d, LANES), row),
            pl.BlockSpec((1, HEAD_DIM), row),
            pl.BlockSpec((1, HEAD_DIM), row),
        ],
        out_specs=[
            pl.BlockSpec((tm, tn), lambda m, n: (m, jnp.minimum(n, n_a - 1))),
            pl.BlockSpec((tm, tn), lambda m, n: (m, jnp.maximum(n - n_a, 0))),
            pl.BlockSpec((tm, LANES), lambda m, n: (m, 0)),
        ],
        out_shape=[
            jax.ShapeDtypeStruct((t, n_a * tn), F32),
            jax.ShapeDtypeStruct((t, (n_tiles - n_a) * tn), BF16),
            jax.ShapeDtypeStruct((t, LANES), F32),
        ],
        scratch_shapes=[pltpu.VMEM((tm, d), BF16)],
        compiler_params=_params(("arbitrary", "arbitrary")),
        name="inproj",
    )(x, gain, scale, shift, w_main, w_gate, qn, kn)


def _gdn_kernel(q_ref, k_ref, v_ref, z_ref, g_ref, cw_ref, alog_ref, dtb_ref, gn_ref, lvl_ref,
                o_ref, s_ref, halo_ref, *, n_heads):
    i = pl.program_id(0)
    tb = q_ref.shape[0]
    c = GDN_CHUNK
    hd = HEAD_DIM
    width = n_heads * hd

    @pl.when(i == 0)
    def _():
        s_ref[...] = jnp.zeros_like(s_ref)
        halo_ref[...] = jnp.zeros_like(halo_ref)

    def conv(ref, idx):
        cur = ref[...]
        xp = jnp.concatenate([halo_ref[idx], cur], axis=0)
        w = cw_ref[:, idx * width:(idx + 1) * width]
        y = xp * w[SHORT_CONV - 1:SHORT_CONV, :]
        for s in range(1, SHORT_CONV):
            y = y + pltpu.roll(xp, s, axis=0) * w[SHORT_CONV - 1 - s:SHORT_CONV - s, :]
        halo_ref[idx] = cur[tb - CONV_HALO:, :]
        return _silu(y[CONV_HALO:, :])

    qc = conv(q_ref, 0)
    kc = conv(k_ref, 1)
    vc = conv(v_ref, 2)

    gl = g_ref[...]
    beta_all = _sigmoid(gl)
    xg = gl + dtb_ref[...]
    softplus = jnp.maximum(xg, 0.0) + jnp.log(1.0 + jnp.exp(-jnp.abs(xg)))
    g_all = -jnp.exp(alog_ref[...]) * softplus

    ii = lax.broadcasted_iota(jnp.int32, (c, c), 0)
    jj = lax.broadcasted_iota(jnp.int32, (c, c), 1)
    incl = ii >= jj
    strict = ii > jj
    tri = jnp.where(incl, 1.0, 0.0).astype(F32)
    eye = jnp.where(ii == jj, 1.0, 0.0).astype(F32)
    gn = gn_ref[...]

    chunks = range(tb // c)
    heads = range(n_heads)
    pairs = [(ch, h) for ch in chunks for h in heads]

    gcs, gts = [], []
    for ch in chunks:
        gc_all = jnp.dot(tri, g_all[ch * c:(ch + 1) * c, :], precision=lax.Precision.HIGHEST,
                         preferred_element_type=F32)
        gcs.append(gc_all)
        gts.append(gc_all.T)

    qs, ks, kbs, vbs, gcols, grows, lmats, amats = {}, {}, {}, {}, {}, {}, {}, {}
    for ch, h in pairs:
        rs = slice(ch * c, (ch + 1) * c)
        cs = slice(h * hd, (h + 1) * hd)
        q = qc[rs, cs]
        k = kc[rs, cs]
        q = q * lax.rsqrt(jnp.sum(q * q, axis=-1, keepdims=True) + EPS) * (hd ** -0.5)
        k = k * lax.rsqrt(jnp.sum(k * k, axis=-1, keepdims=True) + EPS)
        gcol = gcs[ch][:, n_heads + h:n_heads + h + 1]
        grow = gts[ch][n_heads + h:n_heads + h + 1, :]
        bcol = beta_all[rs, h:h + 1]
        decay = jnp.exp(jnp.where(incl, gcol - grow, -jnp.inf))
        kb = k * bcol
        k16 = k.astype(BF16)
        lmats[ch, h] = jnp.where(strict, _dot_nt(kb.astype(BF16), k16) * decay, 0.0)
        amats[ch, h] = jnp.where(incl, _dot_nt(q.astype(BF16), k16) * decay, 0.0).astype(BF16)
        qs[ch, h], ks[ch, h], kbs[ch, h], vbs[ch, h] = q, k, kb, vc[rs, cs] * bcol
        gcols[ch, h], grows[ch, h] = gcol, grow

    tinvs = {p: eye - lmats[p] * lvl_ref[0] for p in pairs}
    for lvl in range(1, lvl_ref.shape[0]):
        t16s = {p: tinvs[p].astype(BF16) for p in pairs}
        ys = {p: _dot((lmats[p] * lvl_ref[lvl]).astype(BF16), t16s[p]).astype(BF16) for p in pairs}
        tinvs = {p: tinvs[p] - _dot(t16s[p], ys[p]) for p in pairs}

    us, ws = {}, {}
    for p in pairs:
        t16 = tinvs[p].astype(BF16)
        us[p] = _dot(t16, vbs[p].astype(BF16))
        ws[p] = _dot(t16, (kbs[p] * jnp.exp(gcols[p])).astype(BF16)).astype(BF16)

    for ch in chunks:
        rs = slice(ch * c, (ch + 1) * c)
        s16 = {h: s_ref[h].astype(BF16) for h in heads}
        vn16 = {h: (us[ch, h] - _dot(ws[ch, h], s16[h])).astype(BF16) for h in heads}
        for h in heads:
            p = (ch, h)
            cs = slice(h * hd, (h + 1) * hd)
            glast = grows[p][:, c - 1:c]
            kdec = ks[p] * jnp.exp(glast - gcols[p])
            s_ref[h] = s_ref[h] * jnp.exp(glast) + _dot_tn(kdec.astype(BF16), vn16[h])
            o = _dot((qs[p] * jnp.exp(gcols[p])).astype(BF16), s16[h]) + _dot(amats[p], vn16[h])
            ms = jnp.mean(o * o, axis=-1, keepdims=True)
            o_ref[rs, cs] = (o * lax.rsqrt(ms + EPS) * gn * _silu(z_ref[rs, cs])).astype(BF16)


def _level_masks(c):
    i = lax.broadcasted_iota(jnp.int32, (c, c), 0)
    j = lax.broadcasted_iota(jnp.int32, (c, c), 1)
    levels = []
    for l in range(c.bit_length() - 1):
        same_big = (i >> (l + 1)) == (j >> (l + 1))
        diff_small = (i >> l) != (j >> l)
        levels.append(jnp.where(same_big & diff_small & (i > j), 1.0, 0.0))
    return jnp.stack(levels).astype(F32)


def _gdn(proj_a, gates, conv_w, alog_pad, dtb_pad, gdn_norm, *, n_heads):
    t = proj_a.shape[0]
    lvl = _level_masks(GDN_CHUNK)
    width = n_heads * HEAD_DIM
    tb = GDN_BLOCK
    row = lambda i: (0, 0)
    return pl.pallas_call(
        functools.partial(_gdn_kernel, n_heads=n_heads),
        grid=(t // tb,),
        in_specs=[
            pl.BlockSpec((tb, width), lambda i: (i, 0)),
            pl.BlockSpec((tb, width), lambda i: (i, 1)),
            pl.BlockSpec((tb, width), lambda i: (i, 2)),
            pl.BlockSpec((tb, width), lambda i: (i, 3)),
            pl.BlockSpec((tb, LANES), lambda i: (i, 0)),
            pl.BlockSpec((SHORT_CONV, 3 * width), row),
            pl.BlockSpec((1, LANES), row),
            pl.BlockSpec((1, LANES), row),
            pl.BlockSpec((1, HEAD_DIM), row),
            pl.BlockSpec(lvl.shape, lambda i: (0, 0, 0)),
        ],
        out_specs=pl.BlockSpec((tb, width), lambda i: (i, 0)),
        out_shape=jax.ShapeDtypeStruct((t, width), BF16),
        scratch_shapes=[
            pltpu.VMEM((n_heads, HEAD_DIM, HEAD_DIM), F32),
            pltpu.VMEM((3, CONV_HALO, width), F32),
        ],
        compiler_params=_params(("arbitrary",)),
        name="gdn",
    )(proj_a, proj_a, proj_a, proj_a, gates, conv_w, alog_pad, dtb_pad, gdn_norm, lvl)


def _sb_kernel(zb_ref, q_ref, k_ref, v_ref, uw_ref, us_ref, o_ref, acc_ref, run_ref, *, group):
    i = pl.program_id(1)
    qb, win, step = SB_QUERY, SB_WINDOW, SB_STEP
    zb = zb_ref[0]
    uw = uw_ref[...]
    row = lax.broadcasted_iota(jnp.int32, (qb, win), 0)
    col = lax.broadcasted_iota(jnp.int32, (qb, win), 1)
    col_s = lax.broadcasted_iota(jnp.int32, (qb, step), 1)

    def log_stay(z):
        return -(jnp.maximum(z, 0.0) + jnp.log(1.0 + jnp.exp(-jnp.abs(z))))

    def rev_cumsum(ls, umat):
        hi, lo = _split16(ls)
        return _dot(hi, umat) + _dot(lo, umat)

    def first_key(g):
        return jnp.maximum((i * group + g + 1) * qb - win, 0)

    maxes = []
    for g in range(group):
        t0 = (i * group + g) * qb
        start = pl.multiple_of(first_key(g), qb)
        q = q_ref[g * qb:(g + 1) * qb, :]
        z = _dot_nt(q, k_ref[pl.ds(start, win), :])
        mask = (start + col) < (t0 + row)
        cum = rev_cumsum(jnp.where(mask, log_stay(z), 0.0), uw)
        a = jnp.where(mask, jnp.exp(z + cum[:, :win]), 0.0)
        acc_ref[g] = _dot(a.astype(BF16), v_ref[pl.ds(start, win), :])
        run = cum[:, win:]
        run_ref[g] = run
        maxes.append(jnp.max(run))

    def block(g, end):
        start = pl.multiple_of(jnp.maximum(end - step, 0), qb)
        z = _dot_nt(q_ref[g * qb:(g + 1) * qb, :], k_ref[pl.ds(start, step), :])
        mask = (start + col_s) < end
        cum = rev_cumsum(jnp.where(mask, log_stay(z), 0.0), us_ref[...])
        run = run_ref[g]
        a = jnp.where(mask, jnp.exp(z + run + cum[:, :step]), 0.0)
        acc_ref[g] += _dot(a.astype(BF16), v_ref[pl.ds(start, step), :])
        run = run + cum[:, step:]
        run_ref[g] = run
        return jnp.max(run)

    def remaining(g, it):
        return first_key(g) - it * step

    def active(g, it, m):
        return jnp.logical_and(remaining(g, it) > 0, m + zb > F32_EXP_ZERO_BELOW)

    def cond(carry):
        it, ms = carry[0], carry[1:]
        go = active(0, it, ms[0])
        for g in range(1, group):
            go = jnp.logical_or(go, active(g, it, ms[g]))
        return go

    def body(carry):
        it, ms = carry[0], carry[1:]
        new = [lax.cond(active(g, it, ms[g]),
                        functools.partial(block, g, remaining(g, it)),
                        lambda m=ms[g]: m)
               for g in range(group)]
        return (it + 1, *new)

    lax.while_loop(cond, body, (jnp.int32(0), *maxes))
    for g in range(group):
        o_ref[g * qb:(g + 1) * qb, :] = acc_ref[g].astype(BF16)


def _rev_cumsum_matrix(n_keys):
    r = lax.broadcasted_iota(jnp.int32, (n_keys, n_keys + LANES), 0)
    c = lax.broadcasted_iota(jnp.int32, (n_keys, n_keys + LANES), 1)
    return jnp.where((r >= c) | (c >= n_keys), 1.0, 0.0).astype(BF16)


def _stick_breaking(proj_b, zbound, *, n_heads):
    t = proj_b.shape[0]
    group = SB_GROUP
    rows = group * SB_QUERY
    assert t % rows == 0 and t >= SB_WINDOW
    uw = _rev_cumsum_matrix(SB_WINDOW)
    us = _rev_cumsum_matrix(SB_STEP)
    const = lambda h, i: (0, 0)
    return pl.pallas_call(
        functools.partial(_sb_kernel, group=group),
        grid=(n_heads, t // rows),
        in_specs=[
            pl.BlockSpec(memory_space=pltpu.SMEM),
            pl.BlockSpec((rows, HEAD_DIM), lambda h, i: (i, h)),
            pl.BlockSpec((t, HEAD_DIM), lambda h, i: (0, n_heads + h)),
            pl.BlockSpec((t, HEAD_DIM), lambda h, i: (0, 2 * n_heads + h)),
            pl.BlockSpec(uw.shape, const),
            pl.BlockSpec(us.shape, const),
        ],
        out_specs=pl.BlockSpec((rows, HEAD_DIM), lambda h, i: (i, h)),
        out_shape=jax.ShapeDtypeStruct((t, n_heads * HEAD_DIM), BF16),
        scratch_shapes=[pltpu.VMEM((group, SB_QUERY, HEAD_DIM), F32),
                        pltpu.VMEM((group, SB_QUERY, LANES), F32)],
        compiler_params=_params(("arbitrary", "arbitrary")),
        name="stick_breaking",
    )(zbound, proj_b, proj_b, proj_b, uw, us)


def _outproj_kernel(x_ref, oa_ref, ob_ref, wa_ref, wb_ref, gate_ref, o_ref):
    y = _dot(oa_ref[...], wa_ref[...]) + _dot(ob_ref[...], wb_ref[...])
    o_ref[...] = x_ref[...] + gate_ref[...] * y


def _outproj(x, o_a, o_b, w_out16, gate, *, tm=512):
    t, d = x.shape
    da = o_a.shape[1]
    db = o_b.shape[1]
    return pl.pallas_call(
        _outproj_kernel,
        grid=(t // tm,),
        in_specs=[
            pl.BlockSpec((tm, d), lambda m: (m, 0)),
            pl.BlockSpec((tm, da), lambda m: (m, 0)),
            pl.BlockSpec((tm, db), lambda m: (m, 0)),
            pl.BlockSpec((da, d), lambda m: (0, 0)),
            pl.BlockSpec((db, d), lambda m: (1, 0)),
            pl.BlockSpec((1, d), lambda m: (0, 0)),
        ],
        out_specs=pl.BlockSpec((tm, d), lambda m: (m, 0)),
        out_shape=jax.ShapeDtypeStruct((t, d), F32),
        compiler_params=_params(("arbitrary",)),
        name="outproj",
    )(x, o_a, o_b, w_out16, w_out16, gate)


def _ffn_kernel(x_ref, halo_ref, g_ref, sc_ref, sh_ref, gate_ref, wg_ref, wv_ref, cw_ref, wd_ref,
                o_ref, h_ref, acc_ref, *, rows):
    m = pl.program_id(0)
    f = pl.program_id(1)
    nf = pl.num_programs(1)
    tm = x_ref.shape[0]

    @pl.when(f == 0)
    def _():
        gain = g_ref[...] * (1.0 + sc_ref[...])
        shift = sh_ref[...]

        def norm(xf):
            ms = jnp.mean(xf * xf, axis=-1, keepdims=True)
            return (xf * lax.rsqrt(ms + EPS) * gain + shift).astype(BF16)

        keep = jnp.where(m > 0, 1.0, 0.0).astype(F32)
        h_ref[0:FFN_HALO, :] = (norm(halo_ref[...]).astype(F32) * keep).astype(BF16)

        def body(r, carry):
            src = pl.ds(pl.multiple_of(r * rows, rows), rows)
            dst = pl.ds(pl.multiple_of(FFN_HALO + r * rows, FFN_HALO), rows)
            h_ref[dst, :] = norm(x_ref[src, :])
            return carry

        lax.fori_loop(0, tm // rows, body, 0)

    ug = _dot(h_ref[...], wg_ref[...])
    cw = cw_ref[...]
    y = ug * cw[FFN_CONV - 1:FFN_CONV, :]
    for s in range(1, FFN_CONV):
        y = y + pltpu.roll(ug, s, axis=0) * cw[FFN_CONV - 1 - s:FFN_CONV - s, :]
    uv = _dot(h_ref[FFN_HALO:, :], wv_ref[...])
    act = (_silu(y[FFN_HALO:, :]) * uv).astype(BF16)
    part = _dot(act, wd_ref[...])

    @pl.when(f == 0)
    def _():
        acc_ref[...] = part

    @pl.when(f > 0)
    def _():
        acc_ref[...] += part

    @pl.when(f == nf - 1)
    def _():
        o_ref[...] = x_ref[...] + gate_ref[...] * acc_ref[...]


def _ffn(x1, gain, scale, shift, gate, w_up16, conv_w, w_down16, *, tm=512, tf=512):
    t, d = x1.shape
    d_ff = w_down16.shape[0]
    nf = d_ff // tf
    row = lambda m, f: (0, 0)
    halo_blocks = tm // FFN_HALO
    return pl.pallas_call(
        functools.partial(_ffn_kernel, rows=128),
        grid=(t // tm, nf),
        in_specs=[
            pl.BlockSpec((tm, d), lambda m, f: (m, 0)),
            pl.BlockSpec((FFN_HALO, d), lambda m, f: (jnp.maximum(m * halo_blocks - 1, 0), 0)),
            pl.BlockSpec((1, d), row),
            pl.BlockSpec((1, d), row),
            pl.BlockSpec((1, d), row),
            pl.BlockSpec((1, d), row),
            pl.BlockSpec((d, tf), lambda m, f: (0, f)),
            pl.BlockSpec((d, tf), lambda m, f: (0, nf + f)),
            pl.BlockSpec((FFN_CONV, tf), lambda m, f: (0, f)),
            pl.BlockSpec((tf, d), lambda m, f: (f, 0)),
        ],
        out_specs=pl.BlockSpec((tm, d), lambda m, f: (m, 0)),
        out_shape=jax.ShapeDtypeStruct((t, d), F32),
        scratch_shapes=[pltpu.VMEM((tm + FFN_HALO, d), BF16), pltpu.VMEM((tm, d), F32)],
        compiler_params=_params(("arbitrary", "arbitrary")),
        name="convffn",
    )(x1, x1, gain, scale, shift, gate, w_up16, w_up16, conv_w, w_down16)


def kernel(x, c, w_ada, b_ada, norm1, w_in, conv_qkv, a_log, dt_bias, gdn_norm, sb_q_norm,
           sb_k_norm, w_out, norm2, w_up, conv_ffn, w_down):
    batch, t, d = x.shape
    assert batch == 1
    depth = w_ada.shape[0]
    n_heads = a_log.shape[1]
    d_gdn = n_heads * HEAD_DIM
    d_sb = (w_in.shape[2] - 4 * d_gdn - 2 * n_heads) // 3
    assert d_sb == d_gdn and 2 * n_heads <= LANES
    o_gate = 4 * d_gdn
    o_sb = o_gate + 2 * n_heads

    mod = _modulation(c, w_ada, b_ada)
    xs = x.reshape(t, d)
    for l in range(depth):
        shift1, scale1, gate1, shift2, scale2, gate2 = [mod[l, :, j * d:(j + 1) * d] for j in range(6)]
        w_main = jnp.concatenate([w_in[l, :, :o_gate], w_in[l, :, o_sb:]], axis=1).astype(BF16)
        w_gate = jnp.pad(w_in[l, :, o_gate:o_sb], ((0, 0), (0, LANES - 2 * n_heads))).astype(BF16)
        proj_a, proj_b, gates = _inproj(
            xs, norm1[l].reshape(1, d), scale1, shift1, w_main, w_gate,
            sb_q_norm[l].reshape(1, HEAD_DIM), sb_k_norm[l].reshape(1, HEAD_DIM))

        pad = (n_heads, LANES - 2 * n_heads)
        alog_pad = jnp.pad(a_log[l], pad).reshape(1, LANES)
        dtb_pad = jnp.pad(dt_bias[l], pad).reshape(1, LANES)
        o_a = _gdn(proj_a, gates, conv_qkv[l], alog_pad, dtb_pad,
                   gdn_norm[l].reshape(1, HEAD_DIM), n_heads=n_heads)

        zbound = (1.02 * HEAD_DIM ** 0.5 * jnp.max(jnp.abs(sb_q_norm[l]))
                  * jnp.max(jnp.abs(sb_k_norm[l]))).reshape(1)
        o_b = _stick_breaking(proj_b, zbound, n_heads=n_heads)

        x1 = _outproj(xs, o_a, o_b, w_out[l].astype(BF16), gate1)
        xs = _ffn(x1, norm2[l].reshape(1, d), scale2, shift2, gate2,
                  w_up[l].astype(BF16), conv_ffn[l], w_down[l].astype(BF16))
    return xs.reshape(batch, t, d)
```

```python
import functools

import jax
import jax.numpy as jnp
from jax import lax
from jax.experimental import pallas as pl
from jax.experimental.pallas import tpu as pltpu

F32 = jnp.float32
BF16 = jnp.bfloat16

HEAD_DIM = 128
EPS = 1e-6
SHORT_CONV = 4
FFN_CONV = 3
LANES = 128
GDN_CHUNK = 128
GDN_BLOCK = 256
CONV_HALO = 8
FFN_HALO = 16
SB_BLOCK = 128
SB_GROUP = 8
F32_EXP_ZERO_BELOW = -104.0
VMEM_LIMIT = 56 * 1024 * 1024


def _params(sem):
    return pltpu.CompilerParams(dimension_semantics=sem, vmem_limit_bytes=VMEM_LIMIT)


def _sigmoid(x):
    return 1.0 / (1.0 + jnp.exp(-x))


def _silu(x):
    return x * _sigmoid(x)


def _dot(a, b):
    return jnp.dot(a, b, preferred_element_type=F32)


def _dot_nt(a, b):
    return lax.dot_general(a, b, (((1,), (1,)), ((), ())), preferred_element_type=F32)


def _split16(a):
    hi = a.astype(BF16)
    return hi, (a - hi.astype(F32)).astype(BF16)


def _dot_tn(a, b):
    return lax.dot_general(a, b, (((0,), (0,)), ((), ())), preferred_element_type=F32)


def _mod_kernel(c_ref, w_ref, b_ref, o_ref):
    c = c_ref[...]
    ca = _silu(c)
    o_ref[0] = jnp.sum(ca * w_ref[0], axis=0, keepdims=True) + b_ref[0]


def _modulation(c, w_ada, b_ada):
    depth, d, n = w_ada.shape
    tn = 1024
    return pl.pallas_call(
        _mod_kernel,
        grid=(depth, n // tn),
        in_specs=[
            pl.BlockSpec((d, 1), lambda l, j: (0, 0)),
            pl.BlockSpec((1, d, tn), lambda l, j: (l, 0, j)),
            pl.BlockSpec((1, 1, tn), lambda l, j: (l, 0, j)),
        ],
        out_specs=pl.BlockSpec((1, 1, tn), lambda l, j: (l, 0, j)),
        out_shape=jax.ShapeDtypeStruct((depth, 1, n), F32),
        compiler_params=_params(("arbitrary", "arbitrary")),
        name="adaln_mod",
    )(c.reshape(d, 1), w_ada, b_ada.reshape(depth, 1, n))


def _inproj_kernel(x_ref, g_ref, sc_ref, sh_ref, w_ref, wg_ref, qn_ref, kn_ref,
                   oa_ref, ob_ref, og_ref, h_ref, *, n_a, rows):
    n = pl.program_id(1)
    tm = x_ref.shape[0]

    @pl.when(n == 0)
    def _():
        gain = g_ref[...] * (1.0 + sc_ref[...])
        shift = sh_ref[...]

        def body(r, carry):
            sl = pl.ds(pl.multiple_of(r * rows, rows), rows)
            xf = x_ref[sl, :]
            ms = jnp.mean(xf * xf, axis=-1, keepdims=True)
            h = xf * lax.rsqrt(ms + EPS) * gain + shift
            h_ref[sl, :] = h.astype(BF16)
            return carry

        lax.fori_loop(0, tm // rows, body, 0)
        og_ref[...] = _dot(h_ref[...], wg_ref[...])

    acc = _dot(h_ref[...], w_ref[...])

    @pl.when(n < n_a)
    def _():
        oa_ref[...] = acc

    def headnorm(gain_ref, mult):
        gain = gain_ref[...] * mult
        for h in range(acc.shape[1] // HEAD_DIM):
            a = acc[:, h * HEAD_DIM:(h + 1) * HEAD_DIM]
            ms = jnp.mean(a * a, axis=-1, keepdims=True)
            ob_ref[:, h * HEAD_DIM:(h + 1) * HEAD_DIM] = (a * lax.rsqrt(ms + EPS) * gain).astype(BF16)

    @pl.when(n == n_a)
    def _():
        headnorm(qn_ref, HEAD_DIM ** -0.5)

    @pl.when(n == n_a + 1)
    def _():
        headnorm(kn_ref, 1.0)

    @pl.when(n == n_a + 2)
    def _():
        ob_ref[...] = acc.astype(BF16)


def _inproj(x, gain, scale, shift, w_main, w_gate, qn, kn, *, tm=512):
    t, d = x.shape
    tn = 1024
    n_a = 4
    n_tiles = w_main.shape[1] // tn
    row = lambda m, n: (0, 0)
    return pl.pallas_call(
        functools.partial(_inproj_kernel, n_a=n_a, rows=128),
        grid=(t // tm, n_tiles),
        in_specs=[
            pl.BlockSpec((tm, d), lambda m, n: (m, 0)),
            pl.BlockSpec((1, d), row),
            pl.BlockSpec((1, d), row),
            pl.BlockSpec((1, d), row),
            pl.BlockSpec((d, tn), lambda m, n: (0, n)),
            pl.BlockSpec((d, LANES), row),
            pl.BlockSpec((1, HEAD_DIM), row),
            pl.BlockSpec((1, HEAD_DIM), row),
        ],
        out_specs=[
            pl.BlockSpec((tm, tn), lambda m, n: (m, jnp.minimum(n, n_a - 1))),
            pl.BlockSpec((tm, tn), lambda m, n: (m, jnp.maximum(n - n_a, 0))),
            pl.BlockSpec((tm, LANES), lambda m, n: (m, 0)),
        ],
        out_shape=[
            jax.ShapeDtypeStruct((t, n_a * tn), F32),
            jax.ShapeDtypeStruct((t, (n_tiles - n_a) * tn), BF16),
            jax.ShapeDtypeStruct((t, LANES), F32),
        ],
        scratch_shapes=[pltpu.VMEM((tm, d), BF16)],
        compiler_params=_params(("arbitrary", "arbitrary")),
        name="inproj",
    )(x, gain, scale, shift, w_main, w_gate, qn, kn)


def _gdn_kernel(q_ref, k_ref, v_ref, z_ref, g_ref, cw_ref, alog_ref, dtb_ref, gn_ref, lvl_ref,
                o_ref, s_ref, halo_ref, *, n_heads):
    i = pl.program_id(0)
    tb = q_ref.shape[0]
    c = GDN_CHUNK
    hd = HEAD_DIM
    width = n_heads * hd

    @pl.when(i == 0)
    def _():
        s_ref[...] = jnp.zeros_like(s_ref)
        halo_ref[...] = jnp.zeros_like(halo_ref)

    def conv(ref, idx):
        cur = ref[...]
        xp = jnp.concatenate([halo_ref[idx], cur], axis=0)
        w = cw_ref[:, idx * width:(idx + 1) * width]
        y = xp * w[SHORT_CONV - 1:SHORT_CONV, :]
        for s in range(1, SHORT_CONV):
            y = y + pltpu.roll(xp, s, axis=0) * w[SHORT_CONV - 1 - s:SHORT_CONV - s, :]
        halo_ref[idx] = cur[tb - CONV_HALO:, :]
        return _silu(y[CONV_HALO:, :])

    qc = conv(q_ref, 0)
    kc = conv(k_ref, 1)
    vc = conv(v_ref, 2)

    gl = g_ref[...]
    beta_all = _sigmoid(gl)
    xg = gl + dtb_ref[...]
    softplus = jnp.maximum(xg, 0.0) + jnp.log(1.0 + jnp.exp(-jnp.abs(xg)))
    g_all = -jnp.exp(alog_ref[...]) * softplus

    ii = lax.broadcasted_iota(jnp.int32, (c, c), 0)
    jj = lax.broadcasted_iota(jnp.int32, (c, c), 1)
    incl = ii >= jj
    strict = ii > jj
    tri = jnp.where(incl, 1.0, 0.0).astype(F32)
    eye = jnp.where(ii == jj, 1.0, 0.0).astype(F32)
    gn = gn_ref[...]

    chunks = range(tb // c)
    heads = range(n_heads)
    pairs = [(ch, h) for ch in chunks for h in heads]

    gcs, gts = [], []
    for ch in chunks:
        gc_all = jnp.dot(tri, g_all[ch * c:(ch + 1) * c, :], precision=lax.Precision.HIGHEST,
                         preferred_element_type=F32)
        gcs.append(gc_all)
        gts.append(gc_all.T)

    qs, ks, kbs, vbs, gcols, grows, lmats, amats = {}, {}, {}, {}, {}, {}, {}, {}
    for ch, h in pairs:
        rs = slice(ch * c, (ch + 1) * c)
        cs = slice(h * hd, (h + 1) * hd)
        q = qc[rs, cs]
        k = kc[rs, cs]
        q = q * lax.rsqrt(jnp.sum(q * q, axis=-1, keepdims=True) + EPS) * (hd ** -0.5)
        k = k * lax.rsqrt(jnp.sum(k * k, axis=-1, keepdims=True) + EPS)
        gcol = gcs[ch][:, n_heads + h:n_heads + h + 1]
        grow = gts[ch][n_heads + h:n_heads + h + 1, :]
        bcol = beta_all[rs, h:h + 1]
        decay = jnp.exp(jnp.where(incl, gcol - grow, -jnp.inf))
        kb = k * bcol
        k16 = k.astype(BF16)
        lmats[ch, h] = jnp.where(strict, _dot_nt(kb.astype(BF16), k16) * decay, 0.0)
        amats[ch, h] = jnp.where(incl, _dot_nt(q.astype(BF16), k16) * decay, 0.0).astype(BF16)
        qs[ch, h], ks[ch, h], kbs[ch, h], vbs[ch, h] = q, k, kb, vc[rs, cs] * bcol
        gcols[ch, h], grows[ch, h] = gcol, grow

    tinvs = {p: eye - lmats[p] * lvl_ref[0] for p in pairs}
    for lvl in range(1, lvl_ref.shape[0]):
        t16s = {p: tinvs[p].astype(BF16) for p in pairs}
        ys = {p: _dot((lmats[p] * lvl_ref[lvl]).astype(BF16), t16s[p]).astype(BF16) for p in pairs}
        tinvs = {p: tinvs[p] - _dot(t16s[p], ys[p]) for p in pairs}

    us, ws = {}, {}
    for p in pairs:
        t16 = tinvs[p].astype(BF16)
        us[p] = _dot(t16, vbs[p].astype(BF16))
        ws[p] = _dot(t16, (kbs[p] * jnp.exp(gcols[p])).astype(BF16)).astype(BF16)

    for ch in chunks:
        rs = slice(ch * c, (ch + 1) * c)
        s16 = {h: s_ref[h].astype(BF16) for h in heads}
        vn16 = {h: (us[ch, h] - _dot(ws[ch, h], s16[h])).astype(BF16) for h in heads}
        for h in heads:
            p = (ch, h)
            cs = slice(h * hd, (h + 1) * hd)
            glast = grows[p][:, c - 1:c]
            kdec = ks[p] * jnp.exp(glast - gcols[p])
            s_ref[h] = s_ref[h] * jnp.exp(glast) + _dot_tn(kdec.astype(BF16), vn16[h])
            o = _dot((qs[p] * jnp.exp(gcols[p])).astype(BF16), s16[h]) + _dot(amats[p], vn16[h])
            ms = jnp.mean(o * o, axis=-1, keepdims=True)
            o_ref[rs, cs] = (o * lax.rsqrt(ms + EPS) * gn * _silu(z_ref[rs, cs])).astype(BF16)


def _level_masks(c):
    i = lax.broadcasted_iota(jnp.int32, (c, c), 0)
    j = lax.broadcasted_iota(jnp.int32, (c, c), 1)
    levels = []
    for l in range(c.bit_length() - 1):
        same_big = (i >> (l + 1)) == (j >> (l + 1))
        diff_small = (i >> l) != (j >> l)
        levels.append(jnp.where(same_big & diff_small & (i > j), 1.0, 0.0))
    return jnp.stack(levels).astype(F32)


def _gdn(proj_a, gates, conv_w, alog_pad, dtb_pad, gdn_norm, *, n_heads):
    t = proj_a.shape[0]
    lvl = _level_masks(GDN_CHUNK)
    width = n_heads * HEAD_DIM
    tb = GDN_BLOCK
    row = lambda i: (0, 0)
    return pl.pallas_call(
        functools.partial(_gdn_kernel, n_heads=n_heads),
        grid=(t // tb,),
        in_specs=[
            pl.BlockSpec((tb, width), lambda i: (i, 0)),
            pl.BlockSpec((tb, width), lambda i: (i, 1)),
            pl.BlockSpec((tb, width), lambda i: (i, 2)),
            pl.BlockSpec((tb, width), lambda i: (i, 3)),
            pl.BlockSpec((tb, LANES), lambda i: (i, 0)),
            pl.BlockSpec((SHORT_CONV, 3 * width), row),
            pl.BlockSpec((1, LANES), row),
            pl.BlockSpec((1, LANES), row),
            pl.BlockSpec((1, HEAD_DIM), row),
            pl.BlockSpec(lvl.shape, lambda i: (0, 0, 0)),
        ],
        out_specs=pl.BlockSpec((tb, width), lambda i: (i, 0)),
        out_shape=jax.ShapeDtypeStruct((t, width), BF16),
        scratch_shapes=[
            pltpu.VMEM((n_heads, HEAD_DIM, HEAD_DIM), F32),
            pltpu.VMEM((3, CONV_HALO, width), F32),
        ],
        compiler_params=_params(("arbitrary",)),
        name="gdn",
    )(proj_a, proj_a, proj_a, proj_a, gates, conv_w, alog_pad, dtb_pad, gdn_norm, lvl)


def _sb_kernel(zb_ref, q_ref, k_ref, v_ref, u_ref, o_ref, acc_ref, run_ref, *, group):
    i = pl.program_id(1)
    b = SB_BLOCK
    zb = zb_ref[0]
    umat = u_ref[...]
    row = lax.broadcasted_iota(jnp.int32, (b, b), 0)
    col = lax.broadcasted_iota(jnp.int32, (b, b), 1)

    def sweep(it, diagonal):
        zs, masks, his, los = [], [], [], []
        for g in range(group):
            j = i * group + g - it
            start = pl.multiple_of(jnp.maximum(j, 0) * b, b)
            z = _dot_nt(q_ref[g * b:(g + 1) * b, :], k_ref[pl.ds(start, b), :])
            mask = (col < row) if diagonal else jnp.broadcast_to(j >= 0, (b, b))
            ls = jnp.where(mask, -(jnp.maximum(z, 0.0) + jnp.log(1.0 + jnp.exp(-jnp.abs(z)))), 0.0)
            hi, lo = _split16(ls)
            zs.append(z), masks.append(mask), his.append(hi), los.append(lo)
        cum = _dot(jnp.concatenate(his + los, axis=0), umat)
        maxes = []
        for g in range(group):
            j = i * group + g - it
            start = pl.multiple_of(jnp.maximum(j, 0) * b, b)
            c = cum[g * b:(g + 1) * b, :] + cum[(group + g) * b:(group + g + 1) * b, :]
            run = run_ref[g]
            a = jnp.where(masks[g], jnp.exp(zs[g] + run + c[:, :b]), 0.0)
            acc_ref[g] += _dot(a.astype(BF16), v_ref[pl.ds(start, b), :])
            run = run + c[:, b:]
            run_ref[g] = run
            maxes.append(jnp.max(run))
        return maxes

    acc_ref[...] = jnp.zeros_like(acc_ref)
    run_ref[...] = jnp.zeros_like(run_ref)
    first = sweep(0, True)

    def cond(carry):
        it, ms = carry[0], carry[1:]
        go = None
        for g in range(group):
            act = jnp.logical_and(i * group + g - it >= 0, ms[g] + zb > F32_EXP_ZERO_BELOW)
            go = act if go is None else jnp.logical_or(go, act)
        return go

    def body(carry):
        it = carry[0]
        return (it + 1, *sweep(it, False))

    lax.while_loop(cond, body, (jnp.int32(1), *first))
    for g in range(group):
        o_ref[g * b:(g + 1) * b, :] = acc_ref[g].astype(BF16)


def _stick_breaking(proj_b, zbound, *, n_heads):
    t = proj_b.shape[0]
    b = SB_BLOCK
    group = SB_GROUP
    rows = group * b
    assert t % rows == 0
    r = lax.broadcasted_iota(jnp.int32, (b, 2 * b), 0)
    c = lax.broadcasted_iota(jnp.int32, (b, 2 * b), 1)
    umat = jnp.where((r >= c) | (c >= b), 1.0, 0.0).astype(BF16)
    return pl.pallas_call(
        functools.partial(_sb_kernel, group=group),
        grid=(n_heads, t // rows),
        in_specs=[
            pl.BlockSpec(memory_space=pltpu.SMEM),
            pl.BlockSpec((rows, HEAD_DIM), lambda h, i: (i, h)),
            pl.BlockSpec((t, HEAD_DIM), lambda h, i: (0, n_heads + h)),
            pl.BlockSpec((t, HEAD_DIM), lambda h, i: (0, 2 * n_heads + h)),
            pl.BlockSpec(umat.shape, lambda h, i: (0, 0)),
        ],
        out_specs=pl.BlockSpec((rows, HEAD_DIM), lambda h, i: (i, h)),
        out_shape=jax.ShapeDtypeStruct((t, n_heads * HEAD_DIM), BF16),
        scratch_shapes=[pltpu.VMEM((group, b, HEAD_DIM), F32), pltpu.VMEM((group, b, b), F32)],
        compiler_params=_params(("arbitrary", "arbitrary")),
        name="stick_breaking",
    )(zbound, proj_b, proj_b, proj_b, umat)


def _outproj_kernel(x_ref, oa_ref, ob_ref, wa_ref, wb_ref, gate_ref, o_ref):
    y = _dot(oa_ref[...], wa_ref[...]) + _dot(ob_ref[...], wb_ref[...])
    o_ref[...] = x_ref[...] + gate_ref[...] * y


def _outproj(x, o_a, o_b, w_out16, gate, *, tm=512):
    t, d = x.shape
    da = o_a.shape[1]
    db = o_b.shape[1]
    return pl.pallas_call(
        _outproj_kernel,
        grid=(t // tm,),
        in_specs=[
            pl.BlockSpec((tm, d), lambda m: (m, 0)),
            pl.BlockSpec((tm, da), lambda m: (m, 0)),
            pl.BlockSpec((tm, db), lambda m: (m, 0)),
            pl.BlockSpec((da, d), lambda m: (0, 0)),
            pl.BlockSpec((db, d), lambda m: (1, 0)),
            pl.BlockSpec((1, d), lambda m: (0, 0)),
        ],
        out_specs=pl.BlockSpec((tm, d), lambda m: (m, 0)),
        out_shape=jax.ShapeDtypeStruct((t, d), F32),
        compiler_params=_params(("arbitrary",)),
        name="outproj",
    )(x, o_a, o_b, w_out16, w_out16, gate)


def _ffn_kernel(x_ref, halo_ref, g_ref, sc_ref, sh_ref, gate_ref, wg_ref, wv_ref, cw_ref, wd_ref,
                o_ref, h_ref, acc_ref, *, rows):
    m = pl.program_id(0)
    f = pl.program_id(1)
    nf = pl.num_programs(1)
    tm = x_ref.shape[0]

    @pl.when(f == 0)
    def _():
        gain = g_ref[...] * (1.0 + sc_ref[...])
        shift = sh_ref[...]

        def norm(xf):
            ms = jnp.mean(xf * xf, axis=-1, keepdims=True)
            return (xf * lax.rsqrt(ms + EPS) * gain + shift).astype(BF16)

        keep = jnp.where(m > 0, 1.0, 0.0).astype(F32)
        h_ref[0:FFN_HALO, :] = (norm(halo_ref[...]).astype(F32) * keep).astype(BF16)

        def body(r, carry):
            src = pl.ds(pl.multiple_of(r * rows, rows), rows)
            dst = pl.ds(pl.multiple_of(FFN_HALO + r * rows, FFN_HALO), rows)
            h_ref[dst, :] = norm(x_ref[src, :])
            return carry

        lax.fori_loop(0, tm // rows, body, 0)

    ug = _dot(h_ref[...], wg_ref[...])
    cw = cw_ref[...]
    y = ug * cw[FFN_CONV - 1:FFN_CONV, :]
    for s in range(1, FFN_CONV):
        y = y + pltpu.roll(ug, s, axis=0) * cw[FFN_CONV - 1 - s:FFN_CONV - s, :]
    uv = _dot(h_ref[FFN_HALO:, :], wv_ref[...])
    act = (_silu(y[FFN_HALO:, :]) * uv).astype(BF16)
    part = _dot(act, wd_ref[...])

    @pl.when(f == 0)
    def _():
        acc_ref[...] = part

    @pl.when(f > 0)
    def _():
        acc_ref[...] += part

    @pl.when(f == nf - 1)
    def _():
        o_ref[...] = x_ref[...] + gate_ref[...] * acc_ref[...]


def _ffn(x1, gain, scale, shift, gate, w_up16, conv_w, w_down16, *, tm=512, tf=512):
    t, d = x1.shape
    d_ff = w_down16.shape[0]
    nf = d_ff // tf
    row = lambda m, f: (0, 0)
    halo_blocks = tm // FFN_HALO
    return pl.pallas_call(
        functools.partial(_ffn_kernel, rows=128),
        grid=(t // tm, nf),
        in_specs=[
            pl.BlockSpec((tm, d), lambda m, f: (m, 0)),
            pl.BlockSpec((FFN_HALO, d), lambda m, f: (jnp.maximum(m * halo_blocks - 1, 0), 0)),
            pl.BlockSpec((1, d), row),
            pl.BlockSpec((1, d), row),
            pl.BlockSpec((1, d), row),
            pl.BlockSpec((1, d), row),
            pl.BlockSpec((d, tf), lambda m, f: (0, f)),
            pl.BlockSpec((d, tf), lambda m, f: (0, nf + f)),
            pl.BlockSpec((FFN_CONV, tf), lambda m, f: (0, f)),
            pl.BlockSpec((tf, d), lambda m, f: (f, 0)),
        ],
        out_specs=pl.BlockSpec((tm, d), lambda m, f: (m, 0)),
        out_shape=jax.ShapeDtypeStruct((t, d), F32),
        scratch_shapes=[pltpu.VMEM((tm + FFN_HALO, d), BF16), pltpu.VMEM((tm, d), F32)],
        compiler_params=_params(("arbitrary", "arbitrary")),
        name="convffn",
    )(x1, x1, gain, scale, shift, gate, w_up16, w_up16, conv_w, w_down16)


def kernel(x, c, w_ada, b_ada, norm1, w_in, conv_qkv, a_log, dt_bias, gdn_norm, sb_q_norm,
           sb_k_norm, w_out, norm2, w_up, conv_ffn, w_down):
    batch, t, d = x.shape
    assert batch == 1
    depth = w_ada.shape[0]
    n_heads = a_log.shape[1]
    d_gdn = n_heads * HEAD_DIM
    d_sb = (w_in.shape[2] - 4 * d_gdn - 2 * n_heads) // 3
    assert d_sb == d_gdn and 2 * n_heads <= LANES
    o_gate = 4 * d_gdn
    o_sb = o_gate + 2 * n_heads

    mod = _modulation(c, w_ada, b_ada)
    xs = x.reshape(t, d)
    for l in range(depth):
        shift1, scale1, gate1, shift2, scale2, gate2 = [mod[l, :, j * d:(j + 1) * d] for j in range(6)]
        w_main = jnp.concatenate([w_in[l, :, :o_gate], w_in[l, :, o_sb:]], axis=1).astype(BF16)
        w_gate = jnp.pad(w_in[l, :, o_gate:o_sb], ((0, 0), (0, LANES - 2 * n_heads))).astype(BF16)
        proj_a, proj_b, gates = _inproj(
            xs, norm1[l].reshape(1, d), scale1, shift1, w_main, w_gate,
            sb_q_norm[l].reshape(1, HEAD_DIM), sb_k_norm[l].reshape(1, HEAD_DIM))

        pad = (n_heads, LANES - 2 * n_heads)
        alog_pad = jnp.pad(a_log[l], pad).reshape(1, LANES)
        dtb_pad = jnp.pad(dt_bias[l], pad).reshape(1, LANES)
        o_a = _gdn(proj_a, gates, conv_qkv[l], alog_pad, dtb_pad,
                   gdn_norm[l].reshape(1, HEAD_DIM), n_heads=n_heads)

        zbound = (1.02 * HEAD_DIM ** 0.5 * jnp.max(jnp.abs(sb_q_norm[l]))
                  * jnp.max(jnp.abs(sb_k_norm[l]))).reshape(1)
        o_b = _stick_breaking(proj_b, zbound, n_heads=n_heads)

        x1 = _outproj(xs, o_a, o_b, w_out[l].astype(BF16), gate1)
        xs = _ffn(x1, norm2[l].reshape(1, d), scale2, shift2, gate2,
                  w_up[l].astype(BF16), conv_ffn[l], w_down[l].astype(BF16))
    return xs.reshape(batch, t, d)
```

```python
import functools

import jax
import jax.numpy as jnp
from jax import lax
from jax.experimental import pallas as pl
from jax.experimental.pallas import tpu as pltpu

F32 = jnp.float32
BF16 = jnp.bfloat16

HEAD_DIM = 128
EPS = 1e-6
SHORT_CONV = 4
FFN_CONV = 3
LANES = 128
GDN_CHUNK = 128
GDN_BLOCK = 256
CONV_HALO = 8
FFN_HALO = 16
SB_BLOCK = 128
SB_GROUP = 8
F32_EXP_ZERO_BELOW = -104.0
VMEM_LIMIT = 56 * 1024 * 1024


def _params(sem):
    return pltpu.CompilerParams(dimension_semantics=sem, vmem_limit_bytes=VMEM_LIMIT)


def _sigmoid(x):
    return 1.0 / (1.0 + jnp.exp(-x))


def _silu(x):
    return x * _sigmoid(x)


def _dot(a, b):
    return jnp.dot(a, b, preferred_element_type=F32)


def _dot_nt(a, b):
    return lax.dot_general(a, b, (((1,), (1,)), ((), ())), preferred_element_type=F32)


def _split16(a):
    hi = a.astype(BF16)
    return hi, (a - hi.astype(F32)).astype(BF16)


def _dot_tn(a, b):
    return lax.dot_general(a, b, (((0,), (0,)), ((), ())), preferred_element_type=F32)


def _mod_kernel(c_ref, w_ref, b_ref, o_ref):
    c = c_ref[...]
    ca = _silu(c)
    o_ref[0] = jnp.sum(ca * w_ref[0], axis=0, keepdims=True) + b_ref[0]


def _modulation(c, w_ada, b_ada):
    depth, d, n = w_ada.shape
    tn = 1024
    return pl.pallas_call(
        _mod_kernel,
        grid=(depth, n // tn),
        in_specs=[
            pl.BlockSpec((d, 1), lambda l, j: (0, 0)),
            pl.BlockSpec((1, d, tn), lambda l, j: (l, 0, j)),
            pl.BlockSpec((1, 1, tn), lambda l, j: (l, 0, j)),
        ],
        out_specs=pl.BlockSpec((1, 1, tn), lambda l, j: (l, 0, j)),
        out_shape=jax.ShapeDtypeStruct((depth, 1, n), F32),
        compiler_params=_params(("arbitrary", "arbitrary")),
        name="adaln_mod",
    )(c.reshape(d, 1), w_ada, b_ada.reshape(depth, 1, n))


def _inproj_kernel(x_ref, g_ref, sc_ref, sh_ref, wa_ref, wb_ref, wg_ref, qn_ref, kn_ref,
                   oa_ref, ob_ref, og_ref, h_ref, *, n_a, rows):
    n = pl.program_id(1)
    tm = x_ref.shape[0]

    @pl.when(n == 0)
    def _():
        gain = g_ref[...] * (1.0 + sc_ref[...])
        shift = sh_ref[...]

        def body(r, carry):
            sl = pl.ds(pl.multiple_of(r * rows, rows), rows)
            xf = x_ref[sl, :]
            ms = jnp.mean(xf * xf, axis=-1, keepdims=True)
            h = xf * lax.rsqrt(ms + EPS) * gain + shift
            h_ref[sl, :] = h.astype(BF16)
            return carry

        lax.fori_loop(0, tm // rows, body, 0)
        og_ref[...] = _dot(h_ref[...], wg_ref[...])

    @pl.when(n < n_a)
    def _():
        oa_ref[...] = _dot(h_ref[...], wa_ref[...])

    @pl.when(n >= n_a)
    def _():
        acc = _dot(h_ref[...], wb_ref[...])

        def headnorm(gain_ref, mult):
            gain = gain_ref[...] * mult
            for h in range(acc.shape[1] // HEAD_DIM):
                a = acc[:, h * HEAD_DIM:(h + 1) * HEAD_DIM]
                ms = jnp.mean(a * a, axis=-1, keepdims=True)
                ob_ref[:, h * HEAD_DIM:(h + 1) * HEAD_DIM] = (a * lax.rsqrt(ms + EPS) * gain).astype(BF16)

        @pl.when(n == n_a)
        def _():
            headnorm(qn_ref, HEAD_DIM ** -0.5)

        @pl.when(n == n_a + 1)
        def _():
            headnorm(kn_ref, 1.0)

        @pl.when(n == n_a + 2)
        def _():
            ob_ref[...] = acc.astype(BF16)


def _inproj(x, gain, scale, shift, w_all, w_sb, w_gate, qn, kn, *, n_a, tm=512):
    t, d = x.shape
    tn = 1024
    n_tiles = n_a + w_sb.shape[1] // tn
    row = lambda m, n: (0, 0)
    return pl.pallas_call(
        functools.partial(_inproj_kernel, n_a=n_a, rows=128),
        grid=(t // tm, n_tiles),
        in_specs=[
            pl.BlockSpec((tm, d), lambda m, n: (m, 0)),
            pl.BlockSpec((1, d), row),
            pl.BlockSpec((1, d), row),
            pl.BlockSpec((1, d), row),
            pl.BlockSpec((d, tn), lambda m, n: (0, jnp.minimum(n, n_a - 1))),
            pl.BlockSpec((d, tn), lambda m, n: (0, jnp.maximum(n - n_a, 0))),
            pl.BlockSpec((d, LANES), row),
            pl.BlockSpec((1, HEAD_DIM), row),
            pl.BlockSpec((1, HEAD_DIM), row),
        ],
        out_specs=[
            pl.BlockSpec((tm, tn), lambda m, n: (m, jnp.minimum(n, n_a - 1))),
            pl.BlockSpec((tm, tn), lambda m, n: (m, jnp.maximum(n - n_a, 0))),
            pl.BlockSpec((tm, LANES), lambda m, n: (m, 0)),
        ],
        out_shape=[
            jax.ShapeDtypeStruct((t, n_a * tn), F32),
            jax.ShapeDtypeStruct((t, (n_tiles - n_a) * tn), BF16),
            jax.ShapeDtypeStruct((t, LANES), F32),
        ],
        scratch_shapes=[pltpu.VMEM((tm, d), BF16)],
        compiler_params=_params(("arbitrary", "arbitrary")),
        name="inproj",
    )(x, gain, scale, shift, w_all, w_sb, w_gate, qn, kn)


def _gdn_kernel(q_ref, k_ref, v_ref, z_ref, g_ref, cw_ref, alog_ref, dtb_ref, gn_ref, lvl_ref,
                o_ref, s_ref, halo_ref, *, n_heads):
    i = pl.program_id(0)
    tb = q_ref.shape[0]
    c = GDN_CHUNK
    hd = HEAD_DIM
    width = n_heads * hd

    @pl.when(i == 0)
    def _():
        s_ref[...] = jnp.zeros_like(s_ref)
        halo_ref[...] = jnp.zeros_like(halo_ref)

    def conv(ref, idx):
        cur = ref[...]
        xp = jnp.concatenate([halo_ref[idx], cur], axis=0)
        w = cw_ref[:, idx * width:(idx + 1) * width]
        y = xp * w[SHORT_CONV - 1:SHORT_CONV, :]
        for s in range(1, SHORT_CONV):
            y = y + pltpu.roll(xp, s, axis=0) * w[SHORT_CONV - 1 - s:SHORT_CONV - s, :]
        halo_ref[idx] = cur[tb - CONV_HALO:, :]
        return _silu(y[CONV_HALO:, :])

    qc = conv(q_ref, 0)
    kc = conv(k_ref, 1)
    vc = conv(v_ref, 2)

    gl = g_ref[...]
    beta_all = _sigmoid(gl)
    xg = gl + dtb_ref[...]
    softplus = jnp.maximum(xg, 0.0) + jnp.log(1.0 + jnp.exp(-jnp.abs(xg)))
    g_all = -jnp.exp(alog_ref[...]) * softplus

    ii = lax.broadcasted_iota(jnp.int32, (c, c), 0)
    jj = lax.broadcasted_iota(jnp.int32, (c, c), 1)
    incl = ii >= jj
    strict = ii > jj
    tri = jnp.where(incl, 1.0, 0.0).astype(F32)
    eye = jnp.where(ii == jj, 1.0, 0.0).astype(F32)
    gn = gn_ref[...]

    chunks = range(tb // c)
    heads = range(n_heads)
    pairs = [(ch, h) for ch in chunks for h in heads]

    gcs, gts = [], []
    for ch in chunks:
        gc_all = jnp.dot(tri, g_all[ch * c:(ch + 1) * c, :], precision=lax.Precision.HIGHEST,
                         preferred_element_type=F32)
        gcs.append(gc_all)
        gts.append(gc_all.T)

    ones16 = jnp.ones((hd, hd), BF16)

    def sumsq(x):
        return _dot((x * x).astype(BF16), ones16)

    qn, kn = {}, {}
    for h in heads:
        cs = slice(h * hd, (h + 1) * hd)
        qn[h] = qc[:, cs] * lax.rsqrt(sumsq(qc[:, cs]) + EPS) * (hd ** -0.5)
        kn[h] = kc[:, cs] * lax.rsqrt(sumsq(kc[:, cs]) + EPS)

    qs, ks, kbs, vbs, gcols, grows, lmats, amats = {}, {}, {}, {}, {}, {}, {}, {}
    for ch, h in pairs:
        rs = slice(ch * c, (ch + 1) * c)
        cs = slice(h * hd, (h + 1) * hd)
        q = qn[h][rs, :]
        k = kn[h][rs, :]
        gcol = gcs[ch][:, n_heads + h:n_heads + h + 1]
        grow = gts[ch][n_heads + h:n_heads + h + 1, :]
        bcol = beta_all[rs, h:h + 1]
        decay = jnp.exp(jnp.where(incl, gcol - grow, -jnp.inf))
        kb = k * bcol
        k16 = k.astype(BF16)
        lmats[ch, h] = jnp.where(strict, _dot_nt(kb.astype(BF16), k16) * decay, 0.0)
        amats[ch, h] = jnp.where(incl, _dot_nt(q.astype(BF16), k16) * decay, 0.0).astype(BF16)
        qs[ch, h], ks[ch, h], kbs[ch, h], vbs[ch, h] = q, k, kb, vc[rs, cs] * bcol
        gcols[ch, h], grows[ch, h] = gcol, grow

    tinvs = {p: eye - lmats[p] * lvl_ref[0] for p in pairs}
    for lvl in range(1, lvl_ref.shape[0]):
        t16s = {p: tinvs[p].astype(BF16) for p in pairs}
        ys = {p: _dot((lmats[p] * lvl_ref[lvl]).astype(BF16), t16s[p]).astype(BF16) for p in pairs}
        tinvs = {p: tinvs[p] - _dot(t16s[p], ys[p]) for p in pairs}

    us, ws = {}, {}
    for p in pairs:
        t16 = tinvs[p].astype(BF16)
        us[p] = _dot(t16, vbs[p].astype(BF16))
        ws[p] = _dot(t16, (kbs[p] * jnp.exp(gcols[p])).astype(BF16)).astype(BF16)

    for ch in chunks:
        rs = slice(ch * c, (ch + 1) * c)
        s16 = {h: s_ref[h].astype(BF16) for h in heads}
        vn16 = {h: (us[ch, h] - _dot(ws[ch, h], s16[h])).astype(BF16) for h in heads}
        for h in heads:
            p = (ch, h)
            cs = slice(h * hd, (h + 1) * hd)
            glast = grows[p][:, c - 1:c]
            kdec = ks[p] * jnp.exp(glast - gcols[p])
            s_ref[h] = s_ref[h] * jnp.exp(glast) + _dot_tn(kdec.astype(BF16), vn16[h])
            o = _dot((qs[p] * jnp.exp(gcols[p])).astype(BF16), s16[h]) + _dot(amats[p], vn16[h])
            ms = sumsq(o) * (1.0 / hd)
            o_ref[rs, cs] = (o * lax.rsqrt(ms + EPS) * gn * _silu(z_ref[rs, cs])).astype(BF16)


def _level_masks(c):
    i = lax.broadcasted_iota(jnp.int32, (c, c), 0)
    j = lax.broadcasted_iota(jnp.int32, (c, c), 1)
    levels = []
    for l in range(c.bit_length() - 1):
        same_big = (i >> (l + 1)) == (j >> (l + 1))
        diff_small = (i >> l) != (j >> l)
        levels.append(jnp.where(same_big & diff_small & (i > j), 1.0, 0.0))
    return jnp.stack(levels).astype(F32)


def _gdn(proj_a, gates, conv_w, alog_pad, dtb_pad, gdn_norm, *, n_heads):
    t = proj_a.shape[0]
    lvl = _level_masks(GDN_CHUNK)
    width = n_heads * HEAD_DIM
    tb = GDN_BLOCK
    row = lambda i: (0, 0)
    return pl.pallas_call(
        functools.partial(_gdn_kernel, n_heads=n_heads),
        grid=(t // tb,),
        in_specs=[
            pl.BlockSpec((tb, width), lambda i: (i, 0)),
            pl.BlockSpec((tb, width), lambda i: (i, 1)),
            pl.BlockSpec((tb, width), lambda i: (i, 2)),
            pl.BlockSpec((tb, width), lambda i: (i, 3)),
            pl.BlockSpec((tb, LANES), lambda i: (i, 0)),
            pl.BlockSpec((SHORT_CONV, 3 * width), row),
            pl.BlockSpec((1, LANES), row),
            pl.BlockSpec((1, LANES), row),
            pl.BlockSpec((1, HEAD_DIM), row),
            pl.BlockSpec(lvl.shape, lambda i: (0, 0, 0)),
        ],
        out_specs=pl.BlockSpec((tb, width), lambda i: (i, 0)),
        out_shape=jax.ShapeDtypeStruct((t, width), BF16),
        scratch_shapes=[
            pltpu.VMEM((n_heads, HEAD_DIM, HEAD_DIM), F32),
            pltpu.VMEM((3, CONV_HALO, width), F32),
        ],
        compiler_params=_params(("arbitrary",)),
        name="gdn",
    )(proj_a, proj_a, proj_a, proj_a, gates, conv_w, alog_pad, dtb_pad, gdn_norm, lvl)


def _sb_kernel(zb_ref, q_ref, k_ref, v_ref, u_ref, o_ref, acc_ref, run_ref, *, group):
    i = pl.program_id(1)
    b = SB_BLOCK
    zb = zb_ref[0]
    umat = u_ref[...]
    row = lax.broadcasted_iota(jnp.int32, (b, b), 0)
    col = lax.broadcasted_iota(jnp.int32, (b, b), 1)

    def sweep(it, diagonal):
        zs, masks, his, los = [], [], [], []
        for g in range(group):
            j = i * group + g - it
            start = pl.multiple_of(jnp.maximum(j, 0) * b, b)
            z = _dot_nt(q_ref[g * b:(g + 1) * b, :], k_ref[pl.ds(start, b), :])
            mask = (col < row) if diagonal else jnp.broadcast_to(j >= 0, (b, b))
            ls = jnp.where(mask, -(jnp.maximum(z, 0.0) + jnp.log(1.0 + jnp.exp(-jnp.abs(z)))), 0.0)
            hi, lo = _split16(ls)
            zs.append(z), masks.append(mask), his.append(hi), los.append(lo)
        cum = _dot(jnp.concatenate(his + los, axis=0), umat)
        maxes = []
        for g in range(group):
            j = i * group + g - it
            start = pl.multiple_of(jnp.maximum(j, 0) * b, b)
            c = cum[g * b:(g + 1) * b, :] + cum[(group + g) * b:(group + g + 1) * b, :]
            run = run_ref[g]
            a = jnp.where(masks[g], jnp.exp(zs[g] + run + c[:, :b]), 0.0)
            acc_ref[g] += _dot(a.astype(BF16), v_ref[pl.ds(start, b), :])
            run = run + c[:, b:]
            run_ref[g] = run
            maxes.append(jnp.max(run))
        return maxes

    acc_ref[...] = jnp.zeros_like(acc_ref)
    run_ref[...] = jnp.zeros_like(run_ref)
    first = sweep(0, True)

    def cond(carry):
        it, ms = carry[0], carry[1:]
        go = None
        for g in range(group):
            act = jnp.logical_and(i * group + g - it >= 0, ms[g] + zb > F32_EXP_ZERO_BELOW)
            go = act if go is None else jnp.logical_or(go, act)
        return go

    def body(carry):
        it = carry[0]
        return (it + 1, *sweep(it, False))

    lax.while_loop(cond, body, (jnp.int32(1), *first))
    for g in range(group):
        o_ref[g * b:(g + 1) * b, :] = acc_ref[g].astype(BF16)


def _stick_breaking(proj_b, zbound, *, n_heads):
    t = proj_b.shape[0]
    b = SB_BLOCK
    group = SB_GROUP
    rows = group * b
    assert t % rows == 0
    r = lax.broadcasted_iota(jnp.int32, (b, 2 * b), 0)
    c = lax.broadcasted_iota(jnp.int32, (b, 2 * b), 1)
    umat = jnp.where((r >= c) | (c >= b), 1.0, 0.0).astype(BF16)
    return pl.pallas_call(
        functools.partial(_sb_kernel, group=group),
        grid=(n_heads, t // rows),
        in_specs=[
            pl.BlockSpec(memory_space=pltpu.SMEM),
            pl.BlockSpec((rows, HEAD_DIM), lambda h, i: (i, h)),
            pl.BlockSpec((t, HEAD_DIM), lambda h, i: (0, n_heads + h)),
            pl.BlockSpec((t, HEAD_DIM), lambda h, i: (0, 2 * n_heads + h)),
            pl.BlockSpec(umat.shape, lambda h, i: (0, 0)),
        ],
        out_specs=pl.BlockSpec((rows, HEAD_DIM), lambda h, i: (i, h)),
        out_shape=jax.ShapeDtypeStruct((t, n_heads * HEAD_DIM), BF16),
        scratch_shapes=[pltpu.VMEM((group, b, HEAD_DIM), F32), pltpu.VMEM((group, b, b), F32)],
        compiler_params=_params(("arbitrary", "arbitrary")),
        name="stick_breaking",
    )(zbound, proj_b, proj_b, proj_b, umat)


def _outproj_kernel(x_ref, oa_ref, ob_ref, wa_ref, wb_ref, gate_ref, o_ref):
    y = _dot(oa_ref[...], wa_ref[...]) + _dot(ob_ref[...], wb_ref[...])
    o_ref[...] = x_ref[...] + gate_ref[...] * y


def _outproj(x, o_a, o_b, w_out16, gate, *, tm=512):
    t, d = x.shape
    da = o_a.shape[1]
    db = o_b.shape[1]
    return pl.pallas_call(
        _outproj_kernel,
        grid=(t // tm,),
        in_specs=[
            pl.BlockSpec((tm, d), lambda m: (m, 0)),
            pl.BlockSpec((tm, da), lambda m: (m, 0)),
            pl.BlockSpec((tm, db), lambda m: (m, 0)),
            pl.BlockSpec((da, d), lambda m: (0, 0)),
            pl.BlockSpec((db, d), lambda m: (1, 0)),
            pl.BlockSpec((1, d), lambda m: (0, 0)),
        ],
        out_specs=pl.BlockSpec((tm, d), lambda m: (m, 0)),
        out_shape=jax.ShapeDtypeStruct((t, d), F32),
        compiler_params=_params(("arbitrary",)),
        name="outproj",
    )(x, o_a, o_b, w_out16, w_out16, gate)


def _ffn_kernel(x_ref, halo_ref, g_ref, sc_ref, sh_ref, gate_ref, wg_ref, wv_ref, cw_ref, wd_ref,
                o_ref, h_ref, acc_ref, *, rows):
    m = pl.program_id(0)
    f = pl.program_id(1)
    nf = pl.num_programs(1)
    tm = x_ref.shape[0]

    @pl.when(f == 0)
    def _():
        gain = g_ref[...] * (1.0 + sc_ref[...])
        shift = sh_ref[...]

        def norm(xf):
            ms = jnp.mean(xf * xf, axis=-1, keepdims=True)
            return (xf * lax.rsqrt(ms + EPS) * gain + shift).astype(BF16)

        keep = jnp.where(m > 0, 1.0, 0.0).astype(F32)
        h_ref[0:FFN_HALO, :] = (norm(halo_ref[...]).astype(F32) * keep).astype(BF16)

        def body(r, carry):
            src = pl.ds(pl.multiple_of(r * rows, rows), rows)
            dst = pl.ds(pl.multiple_of(FFN_HALO + r * rows, FFN_HALO), rows)
            h_ref[dst, :] = norm(x_ref[src, :])
            return carry

        lax.fori_loop(0, tm // rows, body, 0)

    ug = _dot(h_ref[...], wg_ref[...])
    cw = cw_ref[...]
    y = ug * cw[FFN_CONV - 1:FFN_CONV, :]
    for s in range(1, FFN_CONV):
        y = y + pltpu.roll(ug, s, axis=0) * cw[FFN_CONV - 1 - s:FFN_CONV - s, :]
    uv = _dot(h_ref[FFN_HALO:, :], wv_ref[...])
    act = (_silu(y[FFN_HALO:, :]) * uv).astype(BF16)
    part = _dot(act, wd_ref[...])

    @pl.when(f == 0)
    def _():
        acc_ref[...] = part

    @pl.when(f > 0)
    def _():
        acc_ref[...] += part

    @pl.when(f == nf - 1)
    def _():
        o_ref[...] = x_ref[...] + gate_ref[...] * acc_ref[...]


def _ffn(x1, gain, scale, shift, gate, w_up16, conv_w, w_down16, *, tm=512, tf=512):
    t, d = x1.shape
    d_ff = w_down16.shape[0]
    nf = d_ff // tf
    row = lambda m, f: (0, 0)
    halo_blocks = tm // FFN_HALO
    return pl.pallas_call(
        functools.partial(_ffn_kernel, rows=128),
        grid=(t // tm, nf),
        in_specs=[
            pl.BlockSpec((tm, d), lambda m, f: (m, 0)),
            pl.BlockSpec((FFN_HALO, d), lambda m, f: (jnp.maximum(m * halo_blocks - 1, 0), 0)),
            pl.BlockSpec((1, d), row),
            pl.BlockSpec((1, d), row),
            pl.BlockSpec((1, d), row),
            pl.BlockSpec((1, d), row),
            pl.BlockSpec((d, tf), lambda m, f: (0, f)),
            pl.BlockSpec((d, tf), lambda m, f: (0, nf + f)),
            pl.BlockSpec((FFN_CONV, tf), lambda m, f: (0, f)),
            pl.BlockSpec((tf, d), lambda m, f: (f, 0)),
        ],
        out_specs=pl.BlockSpec((tm, d), lambda m, f: (m, 0)),
        out_shape=jax.ShapeDtypeStruct((t, d), F32),
        scratch_shapes=[pltpu.VMEM((tm + FFN_HALO, d), BF16), pltpu.VMEM((tm, d), F32)],
        compiler_params=_params(("arbitrary", "arbitrary")),
        name="convffn",
    )(x1, x1, gain, scale, shift, gate, w_up16, w_up16, conv_w, w_down16)


def kernel(x, c, w_ada, b_ada, norm1, w_in, conv_qkv, a_log, dt_bias, gdn_norm, sb_q_norm,
           sb_k_norm, w_out, norm2, w_up, conv_ffn, w_down):
    batch, t, d = x.shape
    assert batch == 1
    depth = w_ada.shape[0]
    n_heads = a_log.shape[1]
    d_gdn = n_heads * HEAD_DIM
    d_sb = (w_in.shape[2] - 4 * d_gdn - 2 * n_heads) // 3
    assert d_sb == d_gdn and 2 * n_heads <= LANES
    o_gate = 4 * d_gdn
    o_sb = o_gate + 2 * n_heads

    mod = _modulation(c, w_ada, b_ada)
    xs = x.reshape(t, d)
    for l in range(depth):
        shift1, scale1, gate1, shift2, scale2, gate2 = [mod[l, :, j * d:(j + 1) * d] for j in range(6)]
        w_all = w_in[l].astype(BF16)
        w_gate = jnp.pad(w_all[:, o_gate:o_sb], ((0, 0), (0, LANES - 2 * n_heads)))
        proj_a, proj_b, gates = _inproj(
            xs, norm1[l].reshape(1, d), scale1, shift1, w_all, w_all[:, o_sb:], w_gate,
            sb_q_norm[l].reshape(1, HEAD_DIM), sb_k_norm[l].reshape(1, HEAD_DIM), n_a=o_gate // 1024)

        pad = (n_heads, LANES - 2 * n_heads)
        alog_pad = jnp.pad(a_log[l], pad).reshape(1, LANES)
        dtb_pad = jnp.pad(dt_bias[l], pad).reshape(1, LANES)
        o_a = _gdn(proj_a, gates, conv_qkv[l], alog_pad, dtb_pad,
                   gdn_norm[l].reshape(1, HEAD_DIM), n_heads=n_heads)

        zbound = (1.02 * HEAD_DIM ** 0.5 * jnp.max(jnp.abs(sb_q_norm[l]))
                  * jnp.max(jnp.abs(sb_k_norm[l]))).reshape(1)
        o_b = _stick_breaking(proj_b, zbound, n_heads=n_heads)

        x1 = _outproj(xs, o_a, o_b, w_out[l].astype(BF16), gate1)
        xs = _ffn(x1, norm2[l].reshape(1, d), scale2, shift2, gate2,
                  w_up[l].astype(BF16), conv_ffn[l], w_down[l].astype(BF16))
    return xs.reshape(batch, t, d)
```

```python
import functools

import jax
import jax.numpy as jnp
from jax import lax
from jax.experimental import pallas as pl
from jax.experimental.pallas import tpu as pltpu

F32 = jnp.float32
BF16 = jnp.bfloat16

HEAD_DIM = 128
EPS = 1e-6
SHORT_CONV = 4
FFN_CONV = 3
LANES = 128
GDN_CHUNK = 128
GDN_BLOCK = 256
CONV_HALO = 8
FFN_HALO = 16
SB_BLOCK = 128
SB_GROUP = 8
F32_EXP_ZERO_BELOW = -104.0
VMEM_LIMIT = 56 * 1024 * 1024


def _params(sem):
    return pltpu.CompilerParams(dimension_semantics=sem, vmem_limit_bytes=VMEM_LIMIT)


def _sigmoid(x):
    return 1.0 / (1.0 + jnp.exp(-x))


def _silu(x):
    return x * _sigmoid(x)


def _dot(a, b):
    return jnp.dot(a, b, preferred_element_type=F32)


def _dot_nt(a, b):
    return lax.dot_general(a, b, (((1,), (1,)), ((), ())), preferred_element_type=F32)


def _split16(a):
    hi = a.astype(BF16)
    return hi, (a - hi.astype(F32)).astype(BF16)


def _dot_tn(a, b):
    return lax.dot_general(a, b, (((0,), (0,)), ((), ())), preferred_element_type=F32)


def _mod_kernel(c_ref, w_ref, b_ref, o_ref):
    c = c_ref[...]
    ca = _silu(c)
    o_ref[0] = jnp.sum(ca * w_ref[0], axis=0, keepdims=True) + b_ref[0]


def _modulation(c, w_ada, b_ada):
    depth, d, n = w_ada.shape
    tn = 1024
    return pl.pallas_call(
        _mod_kernel,
        grid=(depth, n // tn),
        in_specs=[
            pl.BlockSpec((d, 1), lambda l, j: (0, 0)),
            pl.BlockSpec((1, d, tn), lambda l, j: (l, 0, j)),
            pl.BlockSpec((1, 1, tn), lambda l, j: (l, 0, j)),
        ],
        out_specs=pl.BlockSpec((1, 1, tn), lambda l, j: (l, 0, j)),
        out_shape=jax.ShapeDtypeStruct((depth, 1, n), F32),
        compiler_params=_params(("arbitrary", "arbitrary")),
        name="adaln_mod",
    )(c.reshape(d, 1), w_ada, b_ada.reshape(depth, 1, n))


def _inproj_kernel(x_ref, g_ref, sc_ref, sh_ref, w_ref, wg_ref, qn_ref, kn_ref,
                   oa_ref, ob_ref, og_ref, h_ref, *, n_a, rows):
    n = pl.program_id(1)
    tm = x_ref.shape[0]

    @pl.when(n == 0)
    def _():
        gain = g_ref[...] * (1.0 + sc_ref[...])
        shift = sh_ref[...]

        def body(r, carry):
            sl = pl.ds(pl.multiple_of(r * rows, rows), rows)
            xf = x_ref[sl, :]
            ms = jnp.mean(xf * xf, axis=-1, keepdims=True)
            h = xf * lax.rsqrt(ms + EPS) * gain + shift
            h_ref[sl, :] = h.astype(BF16)
            return carry

        lax.fori_loop(0, tm // rows, body, 0)
        og_ref[...] = _dot(h_ref[...], wg_ref[...])

    acc = _dot(h_ref[...], w_ref[...])

    @pl.when(n < n_a)
    def _():
        oa_ref[...] = acc

    def headnorm(gain_ref, mult):
        gain = gain_ref[...] * mult
        for h in range(acc.shape[1] // HEAD_DIM):
            a = acc[:, h * HEAD_DIM:(h + 1) * HEAD_DIM]
            ms = jnp.mean(a * a, axis=-1, keepdims=True)
            ob_ref[:, h * HEAD_DIM:(h + 1) * HEAD_DIM] = (a * lax.rsqrt(ms + EPS) * gain).astype(BF16)

    @pl.when(n == n_a)
    def _():
        headnorm(qn_ref, HEAD_DIM ** -0.5)

    @pl.when(n == n_a + 1)
    def _():
        headnorm(kn_ref, 1.0)

    @pl.when(n == n_a + 2)
    def _():
        ob_ref[...] = acc.astype(BF16)


def _inproj(x, gain, scale, shift, w_main, w_gate, qn, kn, *, tm=512):
    t, d = x.shape
    tn = 1024
    n_a = 4
    n_tiles = w_main.shape[1] // tn
    row = lambda m, n: (0, 0)
    return pl.pallas_call(
        functools.partial(_inproj_kernel, n_a=n_a, rows=128),
        grid=(t // tm, n_tiles),
        in_specs=[
            pl.BlockSpec((tm, d), lambda m, n: (m, 0)),
            pl.BlockSpec((1, d), row),
            pl.BlockSpec((1, d), row),
            pl.BlockSpec((1, d), row),
            pl.BlockSpec((d, tn), lambda m, n: (0, n)),
            pl.BlockSpec((d, LANES), row),
            pl.BlockSpec((1, HEAD_DIM), row),
            pl.BlockSpec((1, HEAD_DIM), row),
        ],
        out_specs=[
            pl.BlockSpec((tm, tn), lambda m, n: (m, jnp.minimum(n, n_a - 1))),
            pl.BlockSpec((tm, tn), lambda m, n: (m, jnp.maximum(n - n_a, 0))),
            pl.BlockSpec((tm, LANES), lambda m, n: (m, 0)),
        ],
        out_shape=[
            jax.ShapeDtypeStruct((t, n_a * tn), F32),
            jax.ShapeDtypeStruct((t, (n_tiles - n_a) * tn), BF16),
            jax.ShapeDtypeStruct((t, LANES), F32),
        ],
        scratch_shapes=[pltpu.VMEM((tm, d), BF16)],
        compiler_params=_params(("arbitrary", "arbitrary")),
        name="inproj",
    )(x, gain, scale, shift, w_main, w_gate, qn, kn)


def _gdn_kernel(q_ref, k_ref, v_ref, z_ref, g_ref, cw_ref, alog_ref, dtb_ref, gn_ref, lvl_ref,
                o_ref, s_ref, halo_ref, *, n_heads):
    i = pl.program_id(0)
    tb = q_ref.shape[0]
    c = GDN_CHUNK
    hd = HEAD_DIM
    width = n_heads * hd

    @pl.when(i == 0)
    def _():
        s_ref[...] = jnp.zeros_like(s_ref)
        halo_ref[...] = jnp.zeros_like(halo_ref)

    def conv(ref, idx):
        cur = ref[...]
        xp = jnp.concatenate([halo_ref[idx], cur], axis=0)
        w = cw_ref[:, idx * width:(idx + 1) * width]
        y = xp * w[SHORT_CONV - 1:SHORT_CONV, :]
        for s in range(1, SHORT_CONV):
            y = y + pltpu.roll(xp, s, axis=0) * w[SHORT_CONV - 1 - s:SHORT_CONV - s, :]
        halo_ref[idx] = cur[tb - CONV_HALO:, :]
        return _silu(y[CONV_HALO:, :])

    qc = conv(q_ref, 0)
    kc = conv(k_ref, 1)
    vc = conv(v_ref, 2)

    gl = g_ref[...]
    beta_all = _sigmoid(gl)
    xg = gl + dtb_ref[...]
    softplus = jnp.maximum(xg, 0.0) + jnp.log(1.0 + jnp.exp(-jnp.abs(xg)))
    g_all = -jnp.exp(alog_ref[...]) * softplus

    ii = lax.broadcasted_iota(jnp.int32, (c, c), 0)
    jj = lax.broadcasted_iota(jnp.int32, (c, c), 1)
    incl = ii >= jj
    strict = ii > jj
    tri = jnp.where(incl, 1.0, 0.0).astype(F32)
    eye = jnp.where(ii == jj, 1.0, 0.0).astype(F32)
    gn = gn_ref[...]

    chunks = range(tb // c)
    heads = range(n_heads)
    pairs = [(ch, h) for ch in chunks for h in heads]

    gcs, gts = [], []
    for ch in chunks:
        gc_all = jnp.dot(tri, g_all[ch * c:(ch + 1) * c, :], precision=lax.Precision.HIGHEST,
                         preferred_element_type=F32)
        gcs.append(gc_all)
        gts.append(gc_all.T)

    ones16 = jnp.ones((hd, hd), BF16)

    def sumsq(x):
        return _dot((x * x).astype(BF16), ones16)

    qn, kn = {}, {}
    for h in heads:
        cs = slice(h * hd, (h + 1) * hd)
        qn[h] = qc[:, cs] * lax.rsqrt(sumsq(qc[:, cs]) + EPS) * (hd ** -0.5)
        kn[h] = kc[:, cs] * lax.rsqrt(sumsq(kc[:, cs]) + EPS)

    qs, ks, kbs, vbs, gcols, grows, lmats, amats = {}, {}, {}, {}, {}, {}, {}, {}
    for ch, h in pairs:
        rs = slice(ch * c, (ch + 1) * c)
        cs = slice(h * hd, (h + 1) * hd)
        q = qn[h][rs, :]
        k = kn[h][rs, :]
        gcol = gcs[ch][:, n_heads + h:n_heads + h + 1]
        grow = gts[ch][n_heads + h:n_heads + h + 1, :]
        bcol = beta_all[rs, h:h + 1]
        decay = jnp.exp(jnp.where(incl, gcol - grow, -jnp.inf))
        kb = k * bcol
        k16 = k.astype(BF16)
        lmats[ch, h] = jnp.where(strict, _dot_nt(kb.astype(BF16), k16) * decay, 0.0)
        amats[ch, h] = jnp.where(incl, _dot_nt(q.astype(BF16), k16) * decay, 0.0).astype(BF16)
        qs[ch, h], ks[ch, h], kbs[ch, h], vbs[ch, h] = q, k, kb, vc[rs, cs] * bcol
        gcols[ch, h], grows[ch, h] = gcol, grow

    tinvs = {p: eye - lmats[p] * lvl_ref[0] for p in pairs}
    for lvl in range(1, lvl_ref.shape[0]):
        t16s = {p: tinvs[p].astype(BF16) for p in pairs}
        ys = {p: _dot((lmats[p] * lvl_ref[lvl]).astype(BF16), t16s[p]).astype(BF16) for p in pairs}
        tinvs = {p: tinvs[p] - _dot(t16s[p], ys[p]) for p in pairs}

    us, ws = {}, {}
    for p in pairs:
        t16 = tinvs[p].astype(BF16)
        us[p] = _dot(t16, vbs[p].astype(BF16))
        ws[p] = _dot(t16, (kbs[p] * jnp.exp(gcols[p])).astype(BF16)).astype(BF16)

    for ch in chunks:
        rs = slice(ch * c, (ch + 1) * c)
        s16 = {h: s_ref[h].astype(BF16) for h in heads}
        vn16 = {h: (us[ch, h] - _dot(ws[ch, h], s16[h])).astype(BF16) for h in heads}
        for h in heads:
            p = (ch, h)
            cs = slice(h * hd, (h + 1) * hd)
            glast = grows[p][:, c - 1:c]
            kdec = ks[p] * jnp.exp(glast - gcols[p])
            s_ref[h] = s_ref[h] * jnp.exp(glast) + _dot_tn(kdec.astype(BF16), vn16[h])
            o = _dot((qs[p] * jnp.exp(gcols[p])).astype(BF16), s16[h]) + _dot(amats[p], vn16[h])
            ms = sumsq(o) * (1.0 / hd)
            o_ref[rs, cs] = (o * lax.rsqrt(ms + EPS) * gn * _silu(z_ref[rs, cs])).astype(BF16)


def _level_masks(c):
    i = lax.broadcasted_iota(jnp.int32, (c, c), 0)
    j = lax.broadcasted_iota(jnp.int32, (c, c), 1)
    levels = []
    for l in range(c.bit_length() - 1):
        same_big = (i >> (l + 1)) == (j >> (l + 1))
        diff_small = (i >> l) != (j >> l)
        levels.append(jnp.where(same_big & diff_small & (i > j), 1.0, 0.0))
    return jnp.stack(levels).astype(F32)


def _gdn(proj_a, gates, conv_w, alog_pad, dtb_pad, gdn_norm, *, n_heads):
    t = proj_a.shape[0]
    lvl = _level_masks(GDN_CHUNK)
    width = n_heads * HEAD_DIM
    tb = GDN_BLOCK
    row = lambda i: (0, 0)
    return pl.pallas_call(
        functools.partial(_gdn_kernel, n_heads=n_heads),
        grid=(t // tb,),
        in_specs=[
            pl.BlockSpec((tb, width), lambda i: (i, 0)),
            pl.BlockSpec((tb, width), lambda i: (i, 1)),
            pl.BlockSpec((tb, width), lambda i: (i, 2)),
            pl.BlockSpec((tb, width), lambda i: (i, 3)),
            pl.BlockSpec((tb, LANES), lambda i: (i, 0)),
            pl.BlockSpec((SHORT_CONV, 3 * width), row),
            pl.BlockSpec((1, LANES), row),
            pl.BlockSpec((1, LANES), row),
            pl.BlockSpec((1, HEAD_DIM), row),
            pl.BlockSpec(lvl.shape, lambda i: (0, 0, 0)),
        ],
        out_specs=pl.BlockSpec((tb, width), lambda i: (i, 0)),
        out_shape=jax.ShapeDtypeStruct((t, width), BF16),
        scratch_shapes=[
            pltpu.VMEM((n_heads, HEAD_DIM, HEAD_DIM), F32),
            pltpu.VMEM((3, CONV_HALO, width), F32),
        ],
        compiler_params=_params(("arbitrary",)),
        name="gdn",
    )(proj_a, proj_a, proj_a, proj_a, gates, conv_w, alog_pad, dtb_pad, gdn_norm, lvl)


def _sb_kernel(zb_ref, q_ref, k_ref, v_ref, u_ref, o_ref, acc_ref, run_ref, *, group):
    i = pl.program_id(1)
    b = SB_BLOCK
    zb = zb_ref[0]
    umat = u_ref[...]
    row = lax.broadcasted_iota(jnp.int32, (b, b), 0)
    col = lax.broadcasted_iota(jnp.int32, (b, b), 1)

    def sweep(it, diagonal):
        zs, masks, his, los = [], [], [], []
        for g in range(group):
            j = i * group + g - it
            start = pl.multiple_of(jnp.maximum(j, 0) * b, b)
            z = _dot_nt(q_ref[g * b:(g + 1) * b, :], k_ref[pl.ds(start, b), :])
            mask = (col < row) if diagonal else jnp.broadcast_to(j >= 0, (b, b))
            ls = jnp.where(mask, -(jnp.maximum(z, 0.0) + jnp.log(1.0 + jnp.exp(-jnp.abs(z)))), 0.0)
            hi, lo = _split16(ls)
            zs.append(z), masks.append(mask), his.append(hi), los.append(lo)
        cum = _dot(jnp.concatenate(his + los, axis=0), umat)
        maxes = []
        for g in range(group):
            j = i * group + g - it
            start = pl.multiple_of(jnp.maximum(j, 0) * b, b)
            c = cum[g * b:(g + 1) * b, :] + cum[(group + g) * b:(group + g + 1) * b, :]
            run = run_ref[g]
            a = jnp.where(masks[g], jnp.exp(zs[g] + run + c[:, :b]), 0.0)
            acc_ref[g] += _dot(a.astype(BF16), v_ref[pl.ds(start, b), :])
            run = run + c[:, b:]
            run_ref[g] = run
            maxes.append(jnp.max(run))
        return maxes

    acc_ref[...] = jnp.zeros_like(acc_ref)
    run_ref[...] = jnp.zeros_like(run_ref)
    first = sweep(0, True)

    def cond(carry):
        it, ms = carry[0], carry[1:]
        go = None
        for g in range(group):
            act = jnp.logical_and(i * group + g - it >= 0, ms[g] + zb > F32_EXP_ZERO_BELOW)
            go = act if go is None else jnp.logical_or(go, act)
        return go

    def body(carry):
        it = carry[0]
        return (it + 1, *sweep(it, False))

    lax.while_loop(cond, body, (jnp.int32(1), *first))
    for g in range(group):
        o_ref[g * b:(g + 1) * b, :] = acc_ref[g].astype(BF16)


def _stick_breaking(proj_b, zbound, *, n_heads):
    t = proj_b.shape[0]
    b = SB_BLOCK
    group = SB_GROUP
    rows = group * b
    assert t % rows == 0
    r = lax.broadcasted_iota(jnp.int32, (b, 2 * b), 0)
    c = lax.broadcasted_iota(jnp.int32, (b, 2 * b), 1)
    umat = jnp.where((r >= c) | (c >= b), 1.0, 0.0).astype(BF16)
    return pl.pallas_call(
        functools.partial(_sb_kernel, group=group),
        grid=(n_heads, t // rows),
        in_specs=[
            pl.BlockSpec(memory_space=pltpu.SMEM),
            pl.BlockSpec((rows, HEAD_DIM), lambda h, i: (i, h)),
            pl.BlockSpec((t, HEAD_DIM), lambda h, i: (0, n_heads + h)),
            pl.BlockSpec((t, HEAD_DIM), lambda h, i: (0, 2 * n_heads + h)),
            pl.BlockSpec(umat.shape, lambda h, i: (0, 0)),
        ],
        out_specs=pl.BlockSpec((rows, HEAD_DIM), lambda h, i: (i, h)),
        out_shape=jax.ShapeDtypeStruct((t, n_heads * HEAD_DIM), BF16),
        scratch_shapes=[pltpu.VMEM((group, b, HEAD_DIM), F32), pltpu.VMEM((group, b, b), F32)],
        compiler_params=_params(("arbitrary", "arbitrary")),
        name="stick_breaking",
    )(zbound, proj_b, proj_b, proj_b, umat)


def _outproj_kernel(x_ref, oa_ref, ob_ref, wa_ref, wb_ref, gate_ref, o_ref):
    y = _dot(oa_ref[...], wa_ref[...]) + _dot(ob_ref[...], wb_ref[...])
    o_ref[...] = x_ref[...] + gate_ref[...] * y


def _outproj(x, o_a, o_b, w_out16, gate, *, tm=512):
    t, d = x.shape
    da = o_a.shape[1]
    db = o_b.shape[1]
    return pl.pallas_call(
        _outproj_kernel,
        grid=(t // tm,),
        in_specs=[
            pl.BlockSpec((tm, d), lambda m: (m, 0)),
            pl.BlockSpec((tm, da), lambda m: (m, 0)),
            pl.BlockSpec((tm, db), lambda m: (m, 0)),
            pl.BlockSpec((da, d), lambda m: (0, 0)),
            pl.BlockSpec((db, d), lambda m: (1, 0)),
            pl.BlockSpec((1, d), lambda m: (0, 0)),
        ],
        out_specs=pl.BlockSpec((tm, d), lambda m: (m, 0)),
        out_shape=jax.ShapeDtypeStruct((t, d), F32),
        compiler_params=_params(("arbitrary",)),
        name="outproj",
    )(x, o_a, o_b, w_out16, w_out16, gate)


def _ffn_kernel(x_ref, halo_ref, g_ref, sc_ref, sh_ref, gate_ref, wg_ref, wv_ref, cw_ref, wd_ref,
                o_ref, h_ref, acc_ref, *, rows):
    m = pl.program_id(0)
    f = pl.program_id(1)
    nf = pl.num_programs(1)
    tm = x_ref.shape[0]

    @pl.when(f == 0)
    def _():
        gain = g_ref[...] * (1.0 + sc_ref[...])
        shift = sh_ref[...]

        def norm(xf):
            ms = jnp.mean(xf * xf, axis=-1, keepdims=True)
            return (xf * lax.rsqrt(ms + EPS) * gain + shift).astype(BF16)

        keep = jnp.where(m > 0, 1.0, 0.0).astype(F32)
        h_ref[0:FFN_HALO, :] = (norm(halo_ref[...]).astype(F32) * keep).astype(BF16)

        def body(r, carry):
            src = pl.ds(pl.multiple_of(r * rows, rows), rows)
            dst = pl.ds(pl.multiple_of(FFN_HALO + r * rows, FFN_HALO), rows)
            h_ref[dst, :] = norm(x_ref[src, :])
            return carry

        lax.fori_loop(0, tm // rows, body, 0)
        acc_ref[...] = jnp.zeros_like(acc_ref)

    ug = _dot(h_ref[...], wg_ref[...])
    uv = _dot(h_ref[FFN_HALO:, :], wv_ref[...])
    cw = cw_ref[...]
    y = ug * cw[FFN_CONV - 1:FFN_CONV, :]
    for s in range(1, FFN_CONV):
        y = y + pltpu.roll(ug, s, axis=0) * cw[FFN_CONV - 1 - s:FFN_CONV - s, :]
    act = (_silu(y[FFN_HALO:, :]) * uv).astype(BF16)
    acc_ref[...] += _dot(act, wd_ref[...])

    @pl.when(f == nf - 1)
    def _():
        o_ref[...] = x_ref[...] + gate_ref[...] * acc_ref[...]


def _ffn(x1, gain, scale, shift, gate, w_up16, conv_w, w_down16, *, tm=512, tf=512):
    t, d = x1.shape
    d_ff = w_down16.shape[0]
    nf = d_ff // tf
    row = lambda m, f: (0, 0)
    halo_blocks = tm // FFN_HALO
    return pl.pallas_call(
        functools.partial(_ffn_kernel, rows=128),
        grid=(t // tm, nf),
        in_specs=[
            pl.BlockSpec((tm, d), lambda m, f: (m, 0)),
            pl.BlockSpec((FFN_HALO, d), lambda m, f: (jnp.maximum(m * halo_blocks - 1, 0), 0)),
            pl.BlockSpec((1, d), row),
            pl.BlockSpec((1, d), row),
            pl.BlockSpec((1, d), row),
            pl.BlockSpec((1, d), row),
            pl.BlockSpec((d, tf), lambda m, f: (0, f)),
            pl.BlockSpec((d, tf), lambda m, f: (0, nf + f)),
            pl.BlockSpec((FFN_CONV, tf), lambda m, f: (0, f)),
            pl.BlockSpec((tf, d), lambda m, f: (f, 0)),
        ],
        out_specs=pl.BlockSpec((tm, d), lambda m, f: (m, 0)),
        out_shape=jax.ShapeDtypeStruct((t, d), F32),
        scratch_shapes=[pltpu.VMEM((tm + FFN_HALO, d), BF16), pltpu.VMEM((tm, d), F32)],
        compiler_params=_params(("arbitrary", "arbitrary")),
        name="convffn",
    )(x1, x1, gain, scale, shift, gate, w_up16, w_up16, conv_w, w_down16)


def kernel(x, c, w_ada, b_ada, norm1, w_in, conv_qkv, a_log, dt_bias, gdn_norm, sb_q_norm,
           sb_k_norm, w_out, norm2, w_up, conv_ffn, w_down):
    batch, t, d = x.shape
    assert batch == 1
    depth = w_ada.shape[0]
    n_heads = a_log.shape[1]
    d_gdn = n_heads * HEAD_DIM
    d_sb = (w_in.shape[2] - 4 * d_gdn - 2 * n_heads) // 3
    assert d_sb == d_gdn and 2 * n_heads <= LANES
    o_gate = 4 * d_gdn
    o_sb = o_gate + 2 * n_heads

    mod = _modulation(c, w_ada, b_ada)
    xs = x.reshape(t, d)
    for l in range(depth):
        shift1, scale1, gate1, shift2, scale2, gate2 = [mod[l, :, j * d:(j + 1) * d] for j in range(6)]
        w_main = jnp.concatenate([w_in[l, :, :o_gate], w_in[l, :, o_sb:]], axis=1).astype(BF16)
        w_gate = jnp.pad(w_in[l, :, o_gate:o_sb], ((0, 0), (0, LANES - 2 * n_heads))).astype(BF16)
        proj_a, proj_b, gates = _inproj(
            xs, norm1[l].reshape(1, d), scale1, shift1, w_main, w_gate,
            sb_q_norm[l].reshape(1, HEAD_DIM), sb_k_norm[l].reshape(1, HEAD_DIM))

        pad = (n_heads, LANES - 2 * n_heads)
        alog_pad = jnp.pad(a_log[l], pad).reshape(1, LANES)
        dtb_pad = jnp.pad(dt_bias[l], pad).reshape(1, LANES)
        o_a = _gdn(proj_a, gates, conv_qkv[l], alog_pad, dtb_pad,
                   gdn_norm[l].reshape(1, HEAD_DIM), n_heads=n_heads)

        zbound = (1.02 * HEAD_DIM ** 0.5 * jnp.max(jnp.abs(sb_q_norm[l]))
                  * jnp.max(jnp.abs(sb_k_norm[l]))).reshape(1)
        o_b = _stick_breaking(proj_b, zbound, n_heads=n_heads)

        x1 = _outproj(xs, o_a, o_b, w_out[l].astype(BF16), gate1)
        xs = _ffn(x1, norm2[l].reshape(1, d), scale2, shift2, gate2,
                  w_up[l].astype(BF16), conv_ffn[l], w_down[l].astype(BF16))
    return xs.reshape(batch, t, d)
```

```python
import functools

import jax
import jax.numpy as jnp
from jax import lax
from jax.experimental import pallas as pl
from jax.experimental.pallas import tpu as pltpu

F32 = jnp.float32
BF16 = jnp.bfloat16

HEAD_DIM = 128
EPS = 1e-6
SHORT_CONV = 4
FFN_CONV = 3
LANES = 128
GDN_CHUNK = 128
GDN_BLOCK = 256
CONV_HALO = 8
FFN_HALO = 16
SB_BLOCK = 128
SB_GROUP = 16
F32_EXP_ZERO_BELOW = -104.0
VMEM_LIMIT = 56 * 1024 * 1024


def _params(sem):
    return pltpu.CompilerParams(dimension_semantics=sem, vmem_limit_bytes=VMEM_LIMIT)


def _sigmoid(x):
    return 1.0 / (1.0 + jnp.exp(-x))


def _silu(x):
    return x * _sigmoid(x)


def _dot(a, b):
    return jnp.dot(a, b, preferred_element_type=F32)


def _dot_nt(a, b):
    return lax.dot_general(a, b, (((1,), (1,)), ((), ())), preferred_element_type=F32)


def _split16(a):
    hi = a.astype(BF16)
    return hi, (a - hi.astype(F32)).astype(BF16)


def _dot_tn(a, b):
    return lax.dot_general(a, b, (((0,), (0,)), ((), ())), preferred_element_type=F32)


def _mod_kernel(c_ref, w_ref, b_ref, o_ref):
    c = c_ref[...]
    ca = _silu(c)
    o_ref[0] = jnp.sum(ca * w_ref[0], axis=0, keepdims=True) + b_ref[0]


def _modulation(c, w_ada, b_ada):
    depth, d, n = w_ada.shape
    tn = 1024
    return pl.pallas_call(
        _mod_kernel,
        grid=(depth, n // tn),
        in_specs=[
            pl.BlockSpec((d, 1), lambda l, j: (0, 0)),
            pl.BlockSpec((1, d, tn), lambda l, j: (l, 0, j)),
            pl.BlockSpec((1, 1, tn), lambda l, j: (l, 0, j)),
        ],
        out_specs=pl.BlockSpec((1, 1, tn), lambda l, j: (l, 0, j)),
        out_shape=jax.ShapeDtypeStruct((depth, 1, n), F32),
        compiler_params=_params(("arbitrary", "arbitrary")),
        name="adaln_mod",
    )(c.reshape(d, 1), w_ada, b_ada.reshape(depth, 1, n))


def _inproj_kernel(x_ref, g_ref, sc_ref, sh_ref, w_ref, wg_ref, qn_ref, kn_ref,
                   oa_ref, ob_ref, og_ref, h_ref, *, n_a, rows):
    n = pl.program_id(1)
    tm = x_ref.shape[0]

    @pl.when(n == 0)
    def _():
        gain = g_ref[...] * (1.0 + sc_ref[...])
        shift = sh_ref[...]

        def body(r, carry):
            sl = pl.ds(pl.multiple_of(r * rows, rows), rows)
            xf = x_ref[sl, :]
            ms = jnp.mean(xf * xf, axis=-1, keepdims=True)
            h = xf * lax.rsqrt(ms + EPS) * gain + shift
            h_ref[sl, :] = h.astype(BF16)
            return carry

        lax.fori_loop(0, tm // rows, body, 0)
        og_ref[...] = _dot(h_ref[...], wg_ref[...])

    acc = _dot(h_ref[...], w_ref[...])

    @pl.when(n < n_a)
    def _():
        oa_ref[...] = acc

    def headnorm(gain_ref, mult):
        gain = gain_ref[...] * mult
        for h in range(acc.shape[1] // HEAD_DIM):
            a = acc[:, h * HEAD_DIM:(h + 1) * HEAD_DIM]
            ms = jnp.mean(a * a, axis=-1, keepdims=True)
            ob_ref[:, h * HEAD_DIM:(h + 1) * HEAD_DIM] = (a * lax.rsqrt(ms + EPS) * gain).astype(BF16)

    @pl.when(n == n_a)
    def _():
        headnorm(qn_ref, HEAD_DIM ** -0.5)

    @pl.when(n == n_a + 1)
    def _():
        headnorm(kn_ref, 1.0)

    @pl.when(n == n_a + 2)
    def _():
        ob_ref[...] = acc.astype(BF16)


def _inproj(x, gain, scale, shift, w_main, w_gate, qn, kn, *, tm=1024):
    t, d = x.shape
    tn = 1024
    n_a = 4
    n_tiles = w_main.shape[1] // tn
    row = lambda m, n: (0, 0)
    return pl.pallas_call(
        functools.partial(_inproj_kernel, n_a=n_a, rows=128),
        grid=(t // tm, n_tiles),
        in_specs=[
            pl.BlockSpec((tm, d), lambda m, n: (m, 0)),
            pl.BlockSpec((1, d), row),
            pl.BlockSpec((1, d), row),
            pl.BlockSpec((1, d), row),
            pl.BlockSpec((d, tn), lambda m, n: (0, n)),
            pl.BlockSpec((d, LANES), row),
            pl.BlockSpec((1, HEAD_DIM), row),
            pl.BlockSpec((1, HEAD_DIM), row),
        ],
        out_specs=[
            pl.BlockSpec((tm, tn), lambda m, n: (m, jnp.minimum(n, n_a - 1))),
            pl.BlockSpec((tm, tn), lambda m, n: (m, jnp.maximum(n - n_a, 0))),
            pl.BlockSpec((tm, LANES), lambda m, n: (m, 0)),
        ],
        out_shape=[
            jax.ShapeDtypeStruct((t, n_a * tn), F32),
            jax.ShapeDtypeStruct((t, (n_tiles - n_a) * tn), BF16),
            jax.ShapeDtypeStruct((t, LANES), F32),
        ],
        scratch_shapes=[pltpu.VMEM((tm, d), BF16)],
        compiler_params=_params(("arbitrary", "arbitrary")),
        name="inproj",
    )(x, gain, scale, shift, w_main, w_gate, qn, kn)


def _gdn_kernel(q_ref, k_ref, v_ref, z_ref, g_ref, cw_ref, alog_ref, dtb_ref, gn_ref, lvl_ref,
                o_ref, s_ref, halo_ref, *, n_heads):
    i = pl.program_id(0)
    tb = q_ref.shape[0]
    c = GDN_CHUNK
    hd = HEAD_DIM
    width = n_heads * hd

    @pl.when(i == 0)
    def _():
        s_ref[...] = jnp.zeros_like(s_ref)
        halo_ref[...] = jnp.zeros_like(halo_ref)

    def conv(ref, idx):
        cur = ref[...]
        xp = jnp.concatenate([halo_ref[idx], cur], axis=0)
        w = cw_ref[:, idx * width:(idx + 1) * width]
        y = xp * w[SHORT_CONV - 1:SHORT_CONV, :]
        for s in range(1, SHORT_CONV):
            y = y + pltpu.roll(xp, s, axis=0) * w[SHORT_CONV - 1 - s:SHORT_CONV - s, :]
        halo_ref[idx] = cur[tb - CONV_HALO:, :]
        return _silu(y[CONV_HALO:, :])

    qc = conv(q_ref, 0)
    kc = conv(k_ref, 1)
    vc = conv(v_ref, 2)

    gl = g_ref[...]
    beta_all = _sigmoid(gl)
    xg = gl + dtb_ref[...]
    softplus = jnp.maximum(xg, 0.0) + jnp.log(1.0 + jnp.exp(-jnp.abs(xg)))
    g_all = -jnp.exp(alog_ref[...]) * softplus

    ii = lax.broadcasted_iota(jnp.int32, (c, c), 0)
    jj = lax.broadcasted_iota(jnp.int32, (c, c), 1)
    incl = ii >= jj
    strict = ii > jj
    tri = jnp.where(incl, 1.0, 0.0).astype(F32)
    eye = jnp.where(ii == jj, 1.0, 0.0).astype(F32)
    gn = gn_ref[...]

    chunks = range(tb // c)
    heads = range(n_heads)
    pairs = [(ch, h) for ch in chunks for h in heads]

    gcs, gts = [], []
    for ch in chunks:
        gc_all = jnp.dot(tri, g_all[ch * c:(ch + 1) * c, :], precision=lax.Precision.HIGHEST,
                         preferred_element_type=F32)
        gcs.append(gc_all)
        gts.append(gc_all.T)

    ones16 = jnp.ones((hd, hd), BF16)

    def sumsq(x):
        return _dot((x * x).astype(BF16), ones16)

    qn, kn = {}, {}
    for h in heads:
        cs = slice(h * hd, (h + 1) * hd)
        qn[h] = qc[:, cs] * lax.rsqrt(sumsq(qc[:, cs]) + EPS) * (hd ** -0.5)
        kn[h] = kc[:, cs] * lax.rsqrt(sumsq(kc[:, cs]) + EPS)

    qs, ks, kbs, vbs, gcols, grows, lmats, amats = {}, {}, {}, {}, {}, {}, {}, {}
    for ch, h in pairs:
        rs = slice(ch * c, (ch + 1) * c)
        cs = slice(h * hd, (h + 1) * hd)
        q = qn[h][rs, :]
        k = kn[h][rs, :]
        gcol = gcs[ch][:, n_heads + h:n_heads + h + 1]
        grow = gts[ch][n_heads + h:n_heads + h + 1, :]
        bcol = beta_all[rs, h:h + 1]
        decay = jnp.exp(jnp.where(incl, gcol - grow, -jnp.inf))
        kb = k * bcol
        k16 = k.astype(BF16)
        lmats[ch, h] = jnp.where(strict, _dot_nt(kb.astype(BF16), k16) * decay, 0.0)
        amats[ch, h] = jnp.where(incl, _dot_nt(q.astype(BF16), k16) * decay, 0.0).astype(BF16)
        qs[ch, h], ks[ch, h], kbs[ch, h], vbs[ch, h] = q, k, kb, vc[rs, cs] * bcol
        gcols[ch, h], grows[ch, h] = gcol, grow

    tinvs = {p: eye - lmats[p] * lvl_ref[0] for p in pairs}
    for lvl in range(1, lvl_ref.shape[0]):
        t16s = {p: tinvs[p].astype(BF16) for p in pairs}
        ys = {p: _dot((lmats[p] * lvl_ref[lvl]).astype(BF16), t16s[p]).astype(BF16) for p in pairs}
        tinvs = {p: tinvs[p] - _dot(t16s[p], ys[p]) for p in pairs}

    us, ws = {}, {}
    for p in pairs:
        t16 = tinvs[p].astype(BF16)
        us[p] = _dot(t16, vbs[p].astype(BF16))
        ws[p] = _dot(t16, (kbs[p] * jnp.exp(gcols[p])).astype(BF16)).astype(BF16)

    for ch in chunks:
        rs = slice(ch * c, (ch + 1) * c)
        s16 = {h: s_ref[h].astype(BF16) for h in heads}
        vn16 = {h: (us[ch, h] - _dot(ws[ch, h], s16[h])).astype(BF16) for h in heads}
        for h in heads:
            p = (ch, h)
            cs = slice(h * hd, (h + 1) * hd)
            glast = grows[p][:, c - 1:c]
            kdec = ks[p] * jnp.exp(glast - gcols[p])
            s_ref[h] = s_ref[h] * jnp.exp(glast) + _dot_tn(kdec.astype(BF16), vn16[h])
            o = _dot((qs[p] * jnp.exp(gcols[p])).astype(BF16), s16[h]) + _dot(amats[p], vn16[h])
            ms = sumsq(o) * (1.0 / hd)
            o_ref[rs, cs] = (o * lax.rsqrt(ms + EPS) * gn * _silu(z_ref[rs, cs])).astype(BF16)


def _level_masks(c):
    i = lax.broadcasted_iota(jnp.int32, (c, c), 0)
    j = lax.broadcasted_iota(jnp.int32, (c, c), 1)
    levels = []
    for l in range(c.bit_length() - 1):
        same_big = (i >> (l + 1)) == (j >> (l + 1))
        diff_small = (i >> l) != (j >> l)
        levels.append(jnp.where(same_big & diff_small & (i > j), 1.0, 0.0))
    return jnp.stack(levels).astype(F32)


def _gdn(proj_a, gates, conv_w, alog_pad, dtb_pad, gdn_norm, *, n_heads):
    t = proj_a.shape[0]
    lvl = _level_masks(GDN_CHUNK)
    width = n_heads * HEAD_DIM
    tb = GDN_BLOCK
    row = lambda i: (0, 0)
    return pl.pallas_call(
        functools.partial(_gdn_kernel, n_heads=n_heads),
        grid=(t // tb,),
        in_specs=[
            pl.BlockSpec((tb, width), lambda i: (i, 0)),
            pl.BlockSpec((tb, width), lambda i: (i, 1)),
            pl.BlockSpec((tb, width), lambda i: (i, 2)),
            pl.BlockSpec((tb, width), lambda i: (i, 3)),
            pl.BlockSpec((tb, LANES), lambda i: (i, 0)),
            pl.BlockSpec((SHORT_CONV, 3 * width), row),
            pl.BlockSpec((1, LANES), row),
            pl.BlockSpec((1, LANES), row),
            pl.BlockSpec((1, HEAD_DIM), row),
            pl.BlockSpec(lvl.shape, lambda i: (0, 0, 0)),
        ],
        out_specs=pl.BlockSpec((tb, width), lambda i: (i, 0)),
        out_shape=jax.ShapeDtypeStruct((t, width), BF16),
        scratch_shapes=[
            pltpu.VMEM((n_heads, HEAD_DIM, HEAD_DIM), F32),
            pltpu.VMEM((3, CONV_HALO, width), F32),
        ],
        compiler_params=_params(("arbitrary",)),
        name="gdn",
    )(proj_a, proj_a, proj_a, proj_a, gates, conv_w, alog_pad, dtb_pad, gdn_norm, lvl)


def _sb_kernel(zb_ref, q_ref, k_ref, v_ref, u_ref, o_ref, acc_ref, run_ref, *, group):
    i = pl.program_id(1)
    b = SB_BLOCK
    zb = zb_ref[0]
    umat = u_ref[...]
    row = lax.broadcasted_iota(jnp.int32, (b, b), 0)
    col = lax.broadcasted_iota(jnp.int32, (b, b), 1)

    def sweep(it, diagonal):
        zs, masks, his, los = [], [], [], []
        for g in range(group):
            j = i * group + g - it
            start = pl.multiple_of(jnp.maximum(j, 0) * b, b)
            z = _dot_nt(q_ref[g * b:(g + 1) * b, :], k_ref[pl.ds(start, b), :])
            mask = (col < row) if diagonal else jnp.broadcast_to(j >= 0, (b, b))
            ls = jnp.where(mask, -(jnp.maximum(z, 0.0) + jnp.log(1.0 + jnp.exp(-jnp.abs(z)))), 0.0)
            hi, lo = _split16(ls)
            zs.append(z), masks.append(mask), his.append(hi), los.append(lo)
        cum = _dot(jnp.concatenate(his + los, axis=0), umat)
        maxes = []
        for g in range(group):
            j = i * group + g - it
            start = pl.multiple_of(jnp.maximum(j, 0) * b, b)
            c = cum[g * b:(g + 1) * b, :] + cum[(group + g) * b:(group + g + 1) * b, :]
            run = run_ref[g]
            a = jnp.where(masks[g], jnp.exp(zs[g] + run + c[:, :b]), 0.0)
            acc_ref[g] += _dot(a.astype(BF16), v_ref[pl.ds(start, b), :])
            run = run + c[:, b:]
            run_ref[g] = run
            maxes.append(jnp.max(run))
        return maxes

    acc_ref[...] = jnp.zeros_like(acc_ref)
    run_ref[...] = jnp.zeros_like(run_ref)
    first = sweep(0, True)

    def cond(carry):
        it, ms = carry[0], carry[1:]
        go = None
        for g in range(group):
            act = jnp.logical_and(i * group + g - it >= 0, ms[g] + zb > F32_EXP_ZERO_BELOW)
            go = act if go is None else jnp.logical_or(go, act)
        return go

    def body(carry):
        it = carry[0]
        return (it + 1, *sweep(it, False))

    lax.while_loop(cond, body, (jnp.int32(1), *first))
    for g in range(group):
        o_ref[g * b:(g + 1) * b, :] = acc_ref[g].astype(BF16)


def _stick_breaking(proj_b, zbound, *, n_heads):
    t = proj_b.shape[0]
    b = SB_BLOCK
    group = SB_GROUP
    rows = group * b
    assert t % rows == 0
    r = lax.broadcasted_iota(jnp.int32, (b, 2 * b), 0)
    c = lax.broadcasted_iota(jnp.int32, (b, 2 * b), 1)
    umat = jnp.where((r >= c) | (c >= b), 1.0, 0.0).astype(BF16)
    return pl.pallas_call(
        functools.partial(_sb_kernel, group=group),
        grid=(n_heads, t // rows),
        in_specs=[
            pl.BlockSpec(memory_space=pltpu.SMEM),
            pl.BlockSpec((rows, HEAD_DIM), lambda h, i: (i, h)),
            pl.BlockSpec((t, HEAD_DIM), lambda h, i: (0, n_heads + h)),
            pl.BlockSpec((t, HEAD_DIM), lambda h, i: (0, 2 * n_heads + h)),
            pl.BlockSpec(umat.shape, lambda h, i: (0, 0)),
        ],
        out_specs=pl.BlockSpec((rows, HEAD_DIM), lambda h, i: (i, h)),
        out_shape=jax.ShapeDtypeStruct((t, n_heads * HEAD_DIM), BF16),
        scratch_shapes=[pltpu.VMEM((group, b, HEAD_DIM), F32), pltpu.VMEM((group, b, b), F32)],
        compiler_params=_params(("arbitrary", "arbitrary")),
        name="stick_breaking",
    )(zbound, proj_b, proj_b, proj_b, umat)


def _outproj_kernel(x_ref, oa_ref, ob_ref, wa_ref, wb_ref, gate_ref, o_ref):
    y = _dot(oa_ref[...], wa_ref[...]) + _dot(ob_ref[...], wb_ref[...])
    o_ref[...] = x_ref[...] + gate_ref[...] * y


def _outproj(x, o_a, o_b, w_out16, gate, *, tm=512):
    t, d = x.shape
    da = o_a.shape[1]
    db = o_b.shape[1]
    return pl.pallas_call(
        _outproj_kernel,
        grid=(t // tm,),
        in_specs=[
            pl.BlockSpec((tm, d), lambda m: (m, 0)),
            pl.BlockSpec((tm, da), lambda m: (m, 0)),
            pl.BlockSpec((tm, db), lambda m: (m, 0)),
            pl.BlockSpec((da, d), lambda m: (0, 0)),
            pl.BlockSpec((db, d), lambda m: (1, 0)),
            pl.BlockSpec((1, d), lambda m: (0, 0)),
        ],
        out_specs=pl.BlockSpec((tm, d), lambda m: (m, 0)),
        out_shape=jax.ShapeDtypeStruct((t, d), F32),
        compiler_params=_params(("arbitrary",)),
        name="outproj",
    )(x, o_a, o_b, w_out16, w_out16, gate)


def _ffn_kernel(x_ref, halo_ref, g_ref, sc_ref, sh_ref, gate_ref, wg_ref, wv_ref, cw_ref, wd_ref,
                o_ref, h_ref, acc_ref, *, rows):
    m = pl.program_id(0)
    f = pl.program_id(1)
    nf = pl.num_programs(1)
    tm = x_ref.shape[0]

    @pl.when(f == 0)
    def _():
        gain = g_ref[...] * (1.0 + sc_ref[...])
        shift = sh_ref[...]

        def norm(xf):
            ms = jnp.mean(xf * xf, axis=-1, keepdims=True)
            return (xf * lax.rsqrt(ms + EPS) * gain + shift).astype(BF16)

        keep = jnp.where(m > 0, 1.0, 0.0).astype(F32)
        h_ref[0:FFN_HALO, :] = (norm(halo_ref[...]).astype(F32) * keep).astype(BF16)

        def body(r, carry):
            src = pl.ds(pl.multiple_of(r * rows, rows), rows)
            dst = pl.ds(pl.multiple_of(FFN_HALO + r * rows, FFN_HALO), rows)
            h_ref[dst, :] = norm(x_ref[src, :])
            return carry

        lax.fori_loop(0, tm // rows, body, 0)
        acc_ref[...] = jnp.zeros_like(acc_ref)

    ug = _dot(h_ref[...], wg_ref[...])
    uv = _dot(h_ref[FFN_HALO:, :], wv_ref[...])
    cw = cw_ref[...]
    y = ug * cw[FFN_CONV - 1:FFN_CONV, :]
    for s in range(1, FFN_CONV):
        y = y + pltpu.roll(ug, s, axis=0) * cw[FFN_CONV - 1 - s:FFN_CONV - s, :]
    act = (_silu(y[FFN_HALO:, :]) * uv).astype(BF16)
    acc_ref[...] += _dot(act, wd_ref[...])

    @pl.when(f == nf - 1)
    def _():
        o_ref[...] = x_ref[...] + gate_ref[...] * acc_ref[...]


def _ffn(x1, gain, scale, shift, gate, w_up16, conv_w, w_down16, *, tm=512, tf=512):
    t, d = x1.shape
    d_ff = w_down16.shape[0]
    nf = d_ff // tf
    row = lambda m, f: (0, 0)
    halo_blocks = tm // FFN_HALO
    return pl.pallas_call(
        functools.partial(_ffn_kernel, rows=128),
        grid=(t // tm, nf),
        in_specs=[
            pl.BlockSpec((tm, d), lambda m, f: (m, 0)),
            pl.BlockSpec((FFN_HALO, d), lambda m, f: (jnp.maximum(m * halo_blocks - 1, 0), 0)),
            pl.BlockSpec((1, d), row),
            pl.BlockSpec((1, d), row),
            pl.BlockSpec((1, d), row),
            pl.BlockSpec((1, d), row),
            pl.BlockSpec((d, tf), lambda m, f: (0, f)),
            pl.BlockSpec((d, tf), lambda m, f: (0, nf + f)),
            pl.BlockSpec((FFN_CONV, tf), lambda m, f: (0, f)),
            pl.BlockSpec((tf, d), lambda m, f: (f, 0)),
        ],
        out_specs=pl.BlockSpec((tm, d), lambda m, f: (m, 0)),
        out_shape=jax.ShapeDtypeStruct((t, d), F32),
        scratch_shapes=[pltpu.VMEM((tm + FFN_HALO, d), BF16), pltpu.VMEM((tm, d), F32)],
        compiler_params=_params(("arbitrary", "arbitrary")),
        name="convffn",
    )(x1, x1, gain, scale, shift, gate, w_up16, w_up16, conv_w, w_down16)


def kernel(x, c, w_ada, b_ada, norm1, w_in, conv_qkv, a_log, dt_bias, gdn_norm, sb_q_norm,
           sb_k_norm, w_out, norm2, w_up, conv_ffn, w_down):
    batch, t, d = x.shape
    assert batch == 1
    depth = w_ada.shape[0]
    n_heads = a_log.shape[1]
    d_gdn = n_heads * HEAD_DIM
    d_sb = (w_in.shape[2] - 4 * d_gdn - 2 * n_heads) // 3
    assert d_sb == d_gdn and 2 * n_heads <= LANES
    o_gate = 4 * d_gdn
    o_sb = o_gate + 2 * n_heads

    mod = _modulation(c, w_ada, b_ada)
    xs = x.reshape(t, d)
    for l in range(depth):
        shift1, scale1, gate1, shift2, scale2, gate2 = [mod[l, :, j * d:(j + 1) * d] for j in range(6)]
        w_main = jnp.concatenate([w_in[l, :, :o_gate], w_in[l, :, o_sb:]], axis=1).astype(BF16)
        w_gate = jnp.pad(w_in[l, :, o_gate:o_sb], ((0, 0), (0, LANES - 2 * n_heads))).astype(BF16)
        proj_a, proj_b, gates = _inproj(
            xs, norm1[l].reshape(1, d), scale1, shift1, w_main, w_gate,
            sb_q_norm[l].reshape(1, HEAD_DIM), sb_k_norm[l].reshape(1, HEAD_DIM))

        pad = (n_heads, LANES - 2 * n_heads)
        alog_pad = jnp.pad(a_log[l], pad).reshape(1, LANES)
        dtb_pad = jnp.pad(dt_bias[l], pad).reshape(1, LANES)
        o_a = _gdn(proj_a, gates, conv_qkv[l], alog_pad, dtb_pad,
                   gdn_norm[l].reshape(1, HEAD_DIM), n_heads=n_heads)

        zbound = (1.02 * HEAD_DIM ** 0.5 * jnp.max(jnp.abs(sb_q_norm[l]))
                  * jnp.max(jnp.abs(sb_k_norm[l]))).reshape(1)
        o_b = _stick_breaking(proj_b, zbound, n_heads=n_heads)

        x1 = _outproj(xs, o_a, o_b, w_out[l].astype(BF16), gate1)
        xs = _ffn(x1, norm2[l].reshape(1, d), scale2, shift2, gate2,
                  w_up[l].astype(BF16), conv_ffn[l], w_down[l].astype(BF16))
    return xs.reshape(batch, t, d)
```

```python
import functools

import jax
import jax.numpy as jnp
from jax import lax
from jax.experimental import pallas as pl
from jax.experimental.pallas import tpu as pltpu

F32 = jnp.float32
BF16 = jnp.bfloat16

HEAD_DIM = 128
EPS = 1e-6
SHORT_CONV = 4
FFN_CONV = 3
LANES = 128
GDN_CHUNK = 128
GDN_BLOCK = 256
CONV_HALO = 8
FFN_TILE = 512
FFN_HALO = 16
SB_BLOCK = 128
SB_GROUP = 16
F32_EXP_ZERO_BELOW = -104.0
VMEM_LIMIT = 56 * 1024 * 1024


def _params(sem):
    return pltpu.CompilerParams(dimension_semantics=sem, vmem_limit_bytes=VMEM_LIMIT)


def _sigmoid(x):
    return 1.0 / (1.0 + jnp.exp(-x))


def _silu(x):
    return x * _sigmoid(x)


def _dot(a, b):
    return jnp.dot(a, b, preferred_element_type=F32)


def _dot_nt(a, b):
    return lax.dot_general(a, b, (((1,), (1,)), ((), ())), preferred_element_type=F32)


def _split16(a):
    hi = a.astype(BF16)
    return hi, (a - hi.astype(F32)).astype(BF16)


def _dot_tn(a, b):
    return lax.dot_general(a, b, (((0,), (0,)), ((), ())), preferred_element_type=F32)


def _mod_kernel(c_ref, w_ref, b_ref, o_ref):
    c = c_ref[...]
    ca = _silu(c)
    o_ref[0] = jnp.sum(ca * w_ref[0], axis=0, keepdims=True) + b_ref[0]


def _modulation(c, w_ada, b_ada):
    depth, d, n = w_ada.shape
    tn = 1024
    return pl.pallas_call(
        _mod_kernel,
        grid=(depth, n // tn),
        in_specs=[
            pl.BlockSpec((d, 1), lambda l, j: (0, 0)),
            pl.BlockSpec((1, d, tn), lambda l, j: (l, 0, j)),
            pl.BlockSpec((1, 1, tn), lambda l, j: (l, 0, j)),
        ],
        out_specs=pl.BlockSpec((1, 1, tn), lambda l, j: (l, 0, j)),
        out_shape=jax.ShapeDtypeStruct((depth, 1, n), F32),
        compiler_params=_params(("arbitrary", "arbitrary")),
        name="adaln_mod",
    )(c.reshape(d, 1), w_ada, b_ada.reshape(depth, 1, n))


def _inproj_kernel(x_ref, g_ref, sc_ref, sh_ref, w_ref, wg_ref, qn_ref, kn_ref,
                   oa_ref, ob_ref, og_ref, h_ref, *, n_a, rows):
    n = pl.program_id(1)
    tm = x_ref.shape[0]

    @pl.when(n == 0)
    def _():
        gain = g_ref[...] * (1.0 + sc_ref[...])
        shift = sh_ref[...]

        def body(r, carry):
            sl = pl.ds(pl.multiple_of(r * rows, rows), rows)
            xf = x_ref[sl, :]
            ms = jnp.mean(xf * xf, axis=-1, keepdims=True)
            h = xf * lax.rsqrt(ms + EPS) * gain + shift
            h_ref[sl, :] = h.astype(BF16)
            return carry

        lax.fori_loop(0, tm // rows, body, 0)
        og_ref[...] = _dot(h_ref[...], wg_ref[...])

    acc = _dot(h_ref[...], w_ref[...])

    @pl.when(n < n_a)
    def _():
        oa_ref[...] = acc

    def headnorm(gain_ref, mult):
        gain = gain_ref[...] * mult
        for h in range(acc.shape[1] // HEAD_DIM):
            a = acc[:, h * HEAD_DIM:(h + 1) * HEAD_DIM]
            ms = jnp.mean(a * a, axis=-1, keepdims=True)
            ob_ref[:, h * HEAD_DIM:(h + 1) * HEAD_DIM] = (a * lax.rsqrt(ms + EPS) * gain).astype(BF16)

    @pl.when(n == n_a)
    def _():
        headnorm(qn_ref, HEAD_DIM ** -0.5)

    @pl.when(n == n_a + 1)
    def _():
        headnorm(kn_ref, 1.0)

    @pl.when(n == n_a + 2)
    def _():
        ob_ref[...] = acc.astype(BF16)


def _inproj(x, gain, scale, shift, w_main, w_gate, qn, kn, *, tm=1024):
    t, d = x.shape
    tn = 1024
    n_a = 4
    n_tiles = w_main.shape[1] // tn
    row = lambda m, n: (0, 0)
    return pl.pallas_call(
        functools.partial(_inproj_kernel, n_a=n_a, rows=128),
        grid=(t // tm, n_tiles),
        in_specs=[
            pl.BlockSpec((tm, d), lambda m, n: (m, 0)),
            pl.BlockSpec((1, d), row),
            pl.BlockSpec((1, d), row),
            pl.BlockSpec((1, d), row),
            pl.BlockSpec((d, tn), lambda m, n: (0, n)),
            pl.BlockSpec((d, LANES), row),
            pl.BlockSpec((1, HEAD_DIM), row),
            pl.BlockSpec((1, HEAD_DIM), row),
        ],
        out_specs=[
            pl.BlockSpec((tm, tn), lambda m, n: (m, jnp.minimum(n, n_a - 1))),
            pl.BlockSpec((tm, tn), lambda m, n: (m, jnp.maximum(n - n_a, 0))),
            pl.BlockSpec((tm, LANES), lambda m, n: (m, 0)),
        ],
        out_shape=[
            jax.ShapeDtypeStruct((t, n_a * tn), F32),
            jax.ShapeDtypeStruct((t, (n_tiles - n_a) * tn), BF16),
            jax.ShapeDtypeStruct((t, LANES), F32),
        ],
        scratch_shapes=[pltpu.VMEM((tm, d), BF16)],
        compiler_params=_params(("arbitrary", "arbitrary")),
        name="inproj",
    )(x, gain, scale, shift, w_main, w_gate, qn, kn)


def _gdn_kernel(q_ref, k_ref, v_ref, z_ref, g_ref, cw_ref, alog_ref, dtb_ref, gn_ref, lvl_ref,
                o_ref, s_ref, halo_ref, *, n_heads):
    i = pl.program_id(0)
    tb = q_ref.shape[0]
    c = GDN_CHUNK
    hd = HEAD_DIM
    width = n_heads * hd

    @pl.when(i == 0)
    def _():
        s_ref[...] = jnp.zeros_like(s_ref)
        halo_ref[...] = jnp.zeros_like(halo_ref)

    def conv(ref, idx):
        cur = ref[...]
        xp = jnp.concatenate([halo_ref[idx], cur], axis=0)
        w = cw_ref[:, idx * width:(idx + 1) * width]
        y = xp * w[SHORT_CONV - 1:SHORT_CONV, :]
        for s in range(1, SHORT_CONV):
            y = y + pltpu.roll(xp, s, axis=0) * w[SHORT_CONV - 1 - s:SHORT_CONV - s, :]
        halo_ref[idx] = cur[tb - CONV_HALO:, :]
        return _silu(y[CONV_HALO:, :])

    qc = conv(q_ref, 0)
    kc = conv(k_ref, 1)
    vc = conv(v_ref, 2)

    gl = g_ref[...]
    beta_all = _sigmoid(gl)
    xg = gl + dtb_ref[...]
    softplus = jnp.maximum(xg, 0.0) + jnp.log(1.0 + jnp.exp(-jnp.abs(xg)))
    g_all = -jnp.exp(alog_ref[...]) * softplus

    ii = lax.broadcasted_iota(jnp.int32, (c, c), 0)
    jj = lax.broadcasted_iota(jnp.int32, (c, c), 1)
    incl = ii >= jj
    strict = ii > jj
    tri = jnp.where(incl, 1.0, 0.0).astype(F32)
    eye = jnp.where(ii == jj, 1.0, 0.0).astype(F32)
    gn = gn_ref[...]

    chunks = range(tb // c)
    heads = range(n_heads)
    pairs = [(ch, h) for ch in chunks for h in heads]

    gcs, gts = [], []
    for ch in chunks:
        gc_all = jnp.dot(tri, g_all[ch * c:(ch + 1) * c, :], precision=lax.Precision.HIGHEST,
                         preferred_element_type=F32)
        gcs.append(gc_all)
        gts.append(gc_all.T)

    ones16 = jnp.ones((hd, hd), BF16)

    def sumsq(x):
        return _dot((x * x).astype(BF16), ones16)

    qn, kn = {}, {}
    for h in heads:
        cs = slice(h * hd, (h + 1) * hd)
        qn[h] = qc[:, cs] * lax.rsqrt(sumsq(qc[:, cs]) + EPS) * (hd ** -0.5)
        kn[h] = kc[:, cs] * lax.rsqrt(sumsq(kc[:, cs]) + EPS)

    qs, ks, kbs, vbs, gcols, grows, lmats, amats = {}, {}, {}, {}, {}, {}, {}, {}
    for ch, h in pairs:
        rs = slice(ch * c, (ch + 1) * c)
        cs = slice(h * hd, (h + 1) * hd)
        q = qn[h][rs, :]
        k = kn[h][rs, :]
        gcol = gcs[ch][:, n_heads + h:n_heads + h + 1]
        grow = gts[ch][n_heads + h:n_heads + h + 1, :]
        bcol = beta_all[rs, h:h + 1]
        decay = jnp.exp(jnp.where(incl, gcol - grow, -jnp.inf))
        kb = k * bcol
        k16 = k.astype(BF16)
        lmats[ch, h] = jnp.where(strict, _dot_nt(kb.astype(BF16), k16) * decay, 0.0)
        amats[ch, h] = jnp.where(incl, _dot_nt(q.astype(BF16), k16) * decay, 0.0).astype(BF16)
        qs[ch, h], ks[ch, h], kbs[ch, h], vbs[ch, h] = q, k, kb, vc[rs, cs] * bcol
        gcols[ch, h], grows[ch, h] = gcol, grow

    tinvs = {p: eye - lmats[p] * lvl_ref[0] for p in pairs}
    for lvl in range(1, lvl_ref.shape[0]):
        t16s = {p: tinvs[p].astype(BF16) for p in pairs}
        ys = {p: _dot((lmats[p] * lvl_ref[lvl]).astype(BF16), t16s[p]).astype(BF16) for p in pairs}
        tinvs = {p: tinvs[p] - _dot(t16s[p], ys[p]) for p in pairs}

    us, ws = {}, {}
    for p in pairs:
        t16 = tinvs[p].astype(BF16)
        us[p] = _dot(t16, vbs[p].astype(BF16))
        ws[p] = _dot(t16, (kbs[p] * jnp.exp(gcols[p])).astype(BF16)).astype(BF16)

    for ch in chunks:
        rs = slice(ch * c, (ch + 1) * c)
        s16 = {h: s_ref[h].astype(BF16) for h in heads}
        vn16 = {h: (us[ch, h] - _dot(ws[ch, h], s16[h])).astype(BF16) for h in heads}
        for h in heads:
            p = (ch, h)
            cs = slice(h * hd, (h + 1) * hd)
            glast = grows[p][:, c - 1:c]
            kdec = ks[p] * jnp.exp(glast - gcols[p])
            s_ref[h] = s_ref[h] * jnp.exp(glast) + _dot_tn(kdec.astype(BF16), vn16[h])
            o = _dot((qs[p] * jnp.exp(gcols[p])).astype(BF16), s16[h]) + _dot(amats[p], vn16[h])
            ms = sumsq(o) * (1.0 / hd)
            o_ref[rs, cs] = (o * lax.rsqrt(ms + EPS) * gn * _silu(z_ref[rs, cs])).astype(BF16)


def _level_masks(c):
    i = lax.broadcasted_iota(jnp.int32, (c, c), 0)
    j = lax.broadcasted_iota(jnp.int32, (c, c), 1)
    levels = []
    for l in range(c.bit_length() - 1):
        same_big = (i >> (l + 1)) == (j >> (l + 1))
        diff_small = (i >> l) != (j >> l)
        levels.append(jnp.where(same_big & diff_small & (i > j), 1.0, 0.0))
    return jnp.stack(levels).astype(F32)


def _gdn(proj_a, gates, conv_w, alog_pad, dtb_pad, gdn_norm, *, n_heads):
    t = proj_a.shape[0]
    lvl = _level_masks(GDN_CHUNK)
    width = n_heads * HEAD_DIM
    tb = GDN_BLOCK
    row = lambda i: (0, 0)
    return pl.pallas_call(
        functools.partial(_gdn_kernel, n_heads=n_heads),
        grid=(t // tb,),
        in_specs=[
            pl.BlockSpec((tb, width), lambda i: (i, 0)),
            pl.BlockSpec((tb, width), lambda i: (i, 1)),
            pl.BlockSpec((tb, width), lambda i: (i, 2)),
            pl.BlockSpec((tb, width), lambda i: (i, 3)),
            pl.BlockSpec((tb, LANES), lambda i: (i, 0)),
            pl.BlockSpec((SHORT_CONV, 3 * width), row),
            pl.BlockSpec((1, LANES), row),
            pl.BlockSpec((1, LANES), row),
            pl.BlockSpec((1, HEAD_DIM), row),
            pl.BlockSpec(lvl.shape, lambda i: (0, 0, 0)),
        ],
        out_specs=pl.BlockSpec((tb, width), lambda i: (i, 0)),
        out_shape=jax.ShapeDtypeStruct((t, width), BF16),
        scratch_shapes=[
            pltpu.VMEM((n_heads, HEAD_DIM, HEAD_DIM), F32),
            pltpu.VMEM((3, CONV_HALO, width), F32),
        ],
        compiler_params=_params(("arbitrary",)),
        name="gdn",
    )(proj_a, proj_a, proj_a, proj_a, gates, conv_w, alog_pad, dtb_pad, gdn_norm, lvl)


def _sb_kernel(zb_ref, q_ref, k_ref, v_ref, u_ref, o_ref, acc_ref, run_ref, *, group):
    i = pl.program_id(1)
    b = SB_BLOCK
    zb = zb_ref[0]
    umat = u_ref[...]
    row = lax.broadcasted_iota(jnp.int32, (b, b), 0)
    col = lax.broadcasted_iota(jnp.int32, (b, b), 1)

    def sweep(it, diagonal):
        zs, masks, his, los = [], [], [], []
        for g in range(group):
            j = i * group + g - it
            start = pl.multiple_of(jnp.maximum(j, 0) * b, b)
            z = _dot_nt(q_ref[g * b:(g + 1) * b, :], k_ref[pl.ds(start, b), :])
            mask = (col < row) if diagonal else jnp.broadcast_to(j >= 0, (b, b))
            ls = jnp.where(mask, -(jnp.maximum(z, 0.0) + jnp.log(1.0 + jnp.exp(-jnp.abs(z)))), 0.0)
            hi, lo = _split16(ls)
            zs.append(z), masks.append(mask), his.append(hi), los.append(lo)
        cum = _dot(jnp.concatenate(his + los, axis=0), umat)
        maxes = []
        for g in range(group):
            j = i * group + g - it
            start = pl.multiple_of(jnp.maximum(j, 0) * b, b)
            c = cum[g * b:(g + 1) * b, :] + cum[(group + g) * b:(group + g + 1) * b, :]
            run = run_ref[g]
            a = jnp.where(masks[g], jnp.exp(zs[g] + run + c[:, :b]), 0.0)
            acc_ref[g] += _dot(a.astype(BF16), v_ref[pl.ds(start, b), :])
            run = run + c[:, b:]
            run_ref[g] = run
            maxes.append(jnp.max(run))
        return maxes

    acc_ref[...] = jnp.zeros_like(acc_ref)
    run_ref[...] = jnp.zeros_like(run_ref)
    first = sweep(0, True)

    def cond(carry):
        it, ms = carry[0], carry[1:]
        go = None
        for g in range(group):
            act = jnp.logical_and(i * group + g - it >= 0, ms[g] + zb > F32_EXP_ZERO_BELOW)
            go = act if go is None else jnp.logical_or(go, act)
        return go

    def body(carry):
        it = carry[0]
        return (it + 1, *sweep(it, False))

    lax.while_loop(cond, body, (jnp.int32(1), *first))
    for g in range(group):
        o_ref[g * b:(g + 1) * b, :] = acc_ref[g].astype(BF16)


def _stick_breaking(proj_b, zbound, *, n_heads):
    t = proj_b.shape[0]
    b = SB_BLOCK
    group = SB_GROUP
    rows = group * b
    assert t % rows == 0
    r = lax.broadcasted_iota(jnp.int32, (b, 2 * b), 0)
    c = lax.broadcasted_iota(jnp.int32, (b, 2 * b), 1)
    umat = jnp.where((r >= c) | (c >= b), 1.0, 0.0).astype(BF16)
    return pl.pallas_call(
        functools.partial(_sb_kernel, group=group),
        grid=(n_heads, t // rows),
        in_specs=[
            pl.BlockSpec(memory_space=pltpu.SMEM),
            pl.BlockSpec((rows, HEAD_DIM), lambda h, i: (i, h)),
            pl.BlockSpec((t, HEAD_DIM), lambda h, i: (0, n_heads + h)),
            pl.BlockSpec((t, HEAD_DIM), lambda h, i: (0, 2 * n_heads + h)),
            pl.BlockSpec(umat.shape, lambda h, i: (0, 0)),
        ],
        out_specs=pl.BlockSpec((rows, HEAD_DIM), lambda h, i: (i, h)),
        out_shape=jax.ShapeDtypeStruct((t, n_heads * HEAD_DIM), BF16),
        scratch_shapes=[pltpu.VMEM((group, b, HEAD_DIM), F32), pltpu.VMEM((group, b, b), F32)],
        compiler_params=_params(("arbitrary", "arbitrary")),
        name="stick_breaking",
    )(zbound, proj_b, proj_b, proj_b, umat)


def _outproj_kernel(x_ref, oa_ref, ob_ref, wa_ref, wb_ref, gate_ref, o_ref):
    y = _dot(oa_ref[...], wa_ref[...]) + _dot(ob_ref[...], wb_ref[...])
    o_ref[...] = x_ref[...] + gate_ref[...] * y


def _outproj(x, o_a, o_b, w_out16, gate, *, tm=512):
    t, d = x.shape
    da = o_a.shape[1]
    db = o_b.shape[1]
    return pl.pallas_call(
        _outproj_kernel,
        grid=(t // tm,),
        in_specs=[
            pl.BlockSpec((tm, d), lambda m: (m, 0)),
            pl.BlockSpec((tm, da), lambda m: (m, 0)),
            pl.BlockSpec((tm, db), lambda m: (m, 0)),
            pl.BlockSpec((da, d), lambda m: (0, 0)),
            pl.BlockSpec((db, d), lambda m: (1, 0)),
            pl.BlockSpec((1, d), lambda m: (0, 0)),
        ],
        out_specs=pl.BlockSpec((tm, d), lambda m: (m, 0)),
        out_shape=jax.ShapeDtypeStruct((t, d), F32),
        compiler_params=_params(("arbitrary",)),
        name="outproj",
    )(x, o_a, o_b, w_out16, w_out16, gate)


def _ffn_kernel(x_ref, halo_ref, g_ref, sc_ref, sh_ref, gate_ref, wg_ref, wv_ref, cw_ref, wd_ref,
                o_ref, h_ref, acc_ref, *, rows):
    m = pl.program_id(0)
    f = pl.program_id(1)
    nf = pl.num_programs(1)
    tm = x_ref.shape[0]

    @pl.when(f == 0)
    def _():
        gain = g_ref[...] * (1.0 + sc_ref[...])
        shift = sh_ref[...]

        def norm(xf):
            ms = jnp.mean(xf * xf, axis=-1, keepdims=True)
            return (xf * lax.rsqrt(ms + EPS) * gain + shift).astype(BF16)

        keep = jnp.where(m > 0, 1.0, 0.0).astype(F32)
        h_ref[0:FFN_HALO, :] = (norm(halo_ref[...]).astype(F32) * keep).astype(BF16)

        def body(r, carry):
            src = pl.ds(pl.multiple_of(r * rows, rows), rows)
            dst = pl.ds(pl.multiple_of(FFN_HALO + r * rows, FFN_HALO), rows)
            h_ref[dst, :] = norm(x_ref[src, :])
            return carry

        lax.fori_loop(0, tm // rows, body, 0)
        acc_ref[...] = jnp.zeros_like(acc_ref)

    ug = _dot(h_ref[...], wg_ref[...])
    uv = _dot(h_ref[FFN_HALO:, :], wv_ref[...])
    cw = cw_ref[...]
    y = ug * cw[FFN_CONV - 1:FFN_CONV, :]
    for s in range(1, FFN_CONV):
        y = y + pltpu.roll(ug, s, axis=0) * cw[FFN_CONV - 1 - s:FFN_CONV - s, :]
    act = (_silu(y[FFN_HALO:, :]) * uv).astype(BF16)
    acc_ref[...] += _dot(act, wd_ref[...])

    @pl.when(f == nf - 1)
    def _():
        o_ref[...] = x_ref[...] + gate_ref[...] * acc_ref[...]


def _ffn(x1, gain, scale, shift, gate, w_up_tiles, conv_w, w_down16, *, tm=512):
    t, d = x1.shape
    tf = w_up_tiles.shape[2]
    nf = w_up_tiles.shape[0] // 2
    row = lambda m, f: (0, 0)
    halo_blocks = tm // FFN_HALO
    return pl.pallas_call(
        functools.partial(_ffn_kernel, rows=128),
        grid=(t // tm, nf),
        in_specs=[
            pl.BlockSpec((tm, d), lambda m, f: (m, 0)),
            pl.BlockSpec((FFN_HALO, d), lambda m, f: (jnp.maximum(m * halo_blocks - 1, 0), 0)),
            pl.BlockSpec((1, d), row),
            pl.BlockSpec((1, d), row),
            pl.BlockSpec((1, d), row),
            pl.BlockSpec((1, d), row),
            pl.BlockSpec((None, d, tf), lambda m, f: (f, 0, 0)),
            pl.BlockSpec((None, d, tf), lambda m, f: (nf + f, 0, 0)),
            pl.BlockSpec((FFN_CONV, tf), lambda m, f: (0, f)),
            pl.BlockSpec((tf, d), lambda m, f: (f, 0)),
        ],
        out_specs=pl.BlockSpec((tm, d), lambda m, f: (m, 0)),
        out_shape=jax.ShapeDtypeStruct((t, d), F32),
        scratch_shapes=[pltpu.VMEM((tm + FFN_HALO, d), BF16), pltpu.VMEM((tm, d), F32)],
        compiler_params=_params(("arbitrary", "arbitrary")),
        name="convffn",
    )(x1, x1, gain, scale, shift, gate, w_up_tiles, w_up_tiles, conv_w, w_down16)


def kernel(x, c, w_ada, b_ada, norm1, w_in, conv_qkv, a_log, dt_bias, gdn_norm, sb_q_norm,
           sb_k_norm, w_out, norm2, w_up, conv_ffn, w_down):
    batch, t, d = x.shape
    assert batch == 1
    depth = w_ada.shape[0]
    n_heads = a_log.shape[1]
    d_gdn = n_heads * HEAD_DIM
    d_sb = (w_in.shape[2] - 4 * d_gdn - 2 * n_heads) // 3
    assert d_sb == d_gdn and 2 * n_heads <= LANES
    o_gate = 4 * d_gdn
    o_sb = o_gate + 2 * n_heads

    mod = _modulation(c, w_ada, b_ada)
    xs = x.reshape(t, d)
    for l in range(depth):
        shift1, scale1, gate1, shift2, scale2, gate2 = [mod[l, :, j * d:(j + 1) * d] for j in range(6)]
        w_main = jnp.concatenate([w_in[l, :, :o_gate], w_in[l, :, o_sb:]], axis=1).astype(BF16)
        w_gate = jnp.pad(w_in[l, :, o_gate:o_sb], ((0, 0), (0, LANES - 2 * n_heads))).astype(BF16)
        proj_a, proj_b, gates = _inproj(
            xs, norm1[l].reshape(1, d), scale1, shift1, w_main, w_gate,
            sb_q_norm[l].reshape(1, HEAD_DIM), sb_k_norm[l].reshape(1, HEAD_DIM))

        pad = (n_heads, LANES - 2 * n_heads)
        alog_pad = jnp.pad(a_log[l], pad).reshape(1, LANES)
        dtb_pad = jnp.pad(dt_bias[l], pad).reshape(1, LANES)
        o_a = _gdn(proj_a, gates, conv_qkv[l], alog_pad, dtb_pad,
                   gdn_norm[l].reshape(1, HEAD_DIM), n_heads=n_heads)

        zbound = (1.02 * HEAD_DIM ** 0.5 * jnp.max(jnp.abs(sb_q_norm[l]))
                  * jnp.max(jnp.abs(sb_k_norm[l]))).reshape(1)
        o_b = _stick_breaking(proj_b, zbound, n_heads=n_heads)

        x1 = _outproj(xs, o_a, o_b, w_out[l].astype(BF16), gate1)
        w_up_tiles = w_up[l].astype(BF16).reshape(d, -1, FFN_TILE).transpose(1, 0, 2)
        xs = _ffn(x1, norm2[l].reshape(1, d), scale2, shift2, gate2,
                  w_up_tiles, conv_ffn[l], w_down[l].astype(BF16))
    return xs.reshape(batch, t, d)
```

```python
import functools

import jax
import jax.numpy as jnp
from jax import lax
from jax.experimental import pallas as pl
from jax.experimental.pallas import tpu as pltpu

F32 = jnp.float32
BF16 = jnp.bfloat16

HEAD_DIM = 128
EPS = 1e-6
SHORT_CONV = 4
FFN_CONV = 3
LANES = 128
GDN_CHUNK = 128
GDN_BLOCK = 256
CONV_HALO = 8
FFN_HALO = 16
SB_BLOCK = 128
SB_GROUP = 16
F32_EXP_ZERO_BELOW = -104.0
VMEM_LIMIT = 56 * 1024 * 1024


def _params(sem):
    return pltpu.CompilerParams(dimension_semantics=sem, vmem_limit_bytes=VMEM_LIMIT)


def _sigmoid(x):
    return 1.0 / (1.0 + jnp.exp(-x))


def _silu(x):
    return x * _sigmoid(x)


def _dot(a, b):
    return jnp.dot(a, b, preferred_element_type=F32)


def _dot_nt(a, b):
    return lax.dot_general(a, b, (((1,), (1,)), ((), ())), preferred_element_type=F32)


def _split16(a):
    hi = a.astype(BF16)
    return hi, (a - hi.astype(F32)).astype(BF16)


def _dot_tn(a, b):
    return lax.dot_general(a, b, (((0,), (0,)), ((), ())), preferred_element_type=F32)


def _mod_kernel(c_ref, w_ref, b_ref, o_ref):
    c = c_ref[...]
    ca = _silu(c)
    o_ref[0] = jnp.sum(ca * w_ref[0], axis=0, keepdims=True) + b_ref[0]


def _modulation(c, w_ada, b_ada):
    depth, d, n = w_ada.shape
    tn = 1024
    return pl.pallas_call(
        _mod_kernel,
        grid=(depth, n // tn),
        in_specs=[
            pl.BlockSpec((d, 1), lambda l, j: (0, 0)),
            pl.BlockSpec((1, d, tn), lambda l, j: (l, 0, j)),
            pl.BlockSpec((1, 1, tn), lambda l, j: (l, 0, j)),
        ],
        out_specs=pl.BlockSpec((1, 1, tn), lambda l, j: (l, 0, j)),
        out_shape=jax.ShapeDtypeStruct((depth, 1, n), F32),
        compiler_params=_params(("arbitrary", "arbitrary")),
        name="adaln_mod",
    )(c.reshape(d, 1), w_ada, b_ada.reshape(depth, 1, n))


def _inproj_kernel(x_ref, g_ref, sc_ref, sh_ref, w_ref, wg_ref, qn_ref, kn_ref,
                   oa_ref, ob_ref, og_ref, h_ref, *, n_a, rows):
    n = pl.program_id(1)
    tm = x_ref.shape[0]

    @pl.when(n == 0)
    def _():
        gain = g_ref[...] * (1.0 + sc_ref[...])
        shift = sh_ref[...]

        def body(r, carry):
            sl = pl.ds(pl.multiple_of(r * rows, rows), rows)
            xf = x_ref[sl, :]
            ms = jnp.mean(xf * xf, axis=-1, keepdims=True)
            h = xf * lax.rsqrt(ms + EPS) * gain + shift
            h_ref[sl, :] = h.astype(BF16)
            return carry

        lax.fori_loop(0, tm // rows, body, 0)
        og_ref[...] = _dot(h_ref[...], wg_ref[...])

    acc = _dot(h_ref[...], w_ref[...])

    @pl.when(n < n_a)
    def _():
        oa_ref[...] = acc

    def headnorm(gain_ref, mult):
        gain = gain_ref[...] * mult
        for h in range(acc.shape[1] // HEAD_DIM):
            a = acc[:, h * HEAD_DIM:(h + 1) * HEAD_DIM]
            ms = jnp.mean(a * a, axis=-1, keepdims=True)
            ob_ref[:, h * HEAD_DIM:(h + 1) * HEAD_DIM] = (a * lax.rsqrt(ms + EPS) * gain).astype(BF16)

    @pl.when(n == n_a)
    def _():
        headnorm(qn_ref, HEAD_DIM ** -0.5)

    @pl.when(n == n_a + 1)
    def _():
        headnorm(kn_ref, 1.0)

    @pl.when(n == n_a + 2)
    def _():
        ob_ref[...] = acc.astype(BF16)


def _inproj(x, gain, scale, shift, w_main, w_gate, qn, kn, *, tm=1024):
    t, d = x.shape
    tn = 1024
    n_a = 4
    n_tiles = w_main.shape[1] // tn
    row = lambda m, n: (0, 0)
    return pl.pallas_call(
        functools.partial(_inproj_kernel, n_a=n_a, rows=128),
        grid=(t // tm, n_tiles),
        in_specs=[
            pl.BlockSpec((tm, d), lambda m, n: (m, 0)),
            pl.BlockSpec((1, d), row),
            pl.BlockSpec((1, d), row),
            pl.BlockSpec((1, d), row),
            pl.BlockSpec((d, tn), lambda m, n: (0, n)),
            pl.BlockSpec((d, LANES), row),
            pl.BlockSpec((1, HEAD_DIM), row),
            pl.BlockSpec((1, HEAD_DIM), row),
        ],
        out_specs=[
            pl.BlockSpec((tm, tn), lambda m, n: (m, jnp.minimum(n, n_a - 1))),
            pl.BlockSpec((tm, tn), lambda m, n: (m, jnp.maximum(n - n_a, 0))),
            pl.BlockSpec((tm, LANES), lambda m, n: (m, 0)),
        ],
        out_shape=[
            jax.ShapeDtypeStruct((t, n_a * tn), F32),
            jax.ShapeDtypeStruct((t, (n_tiles - n_a) * tn), BF16),
            jax.ShapeDtypeStruct((t, LANES), F32),
        ],
        scratch_shapes=[pltpu.VMEM((tm, d), BF16)],
        compiler_params=_params(("arbitrary", "arbitrary")),
        name="inproj",
    )(x, gain, scale, shift, w_main, w_gate, qn, kn)


def _gdn_kernel(q_ref, k_ref, v_ref, z_ref, g_ref, cw_ref, alog_ref, dtb_ref, gn_ref, lvl_ref,
                o_ref, s_ref, halo_ref, *, n_heads):
    i = pl.program_id(0)
    tb = q_ref.shape[0]
    c = GDN_CHUNK
    hd = HEAD_DIM
    width = n_heads * hd

    @pl.when(i == 0)
    def _():
        s_ref[...] = jnp.zeros_like(s_ref)
        halo_ref[...] = jnp.zeros_like(halo_ref)

    def conv(ref, idx):
        cur = ref[...]
        xp = jnp.concatenate([halo_ref[idx], cur], axis=0)
        w = cw_ref[:, idx * width:(idx + 1) * width]
        y = xp * w[SHORT_CONV - 1:SHORT_CONV, :]
        for s in range(1, SHORT_CONV):
            y = y + pltpu.roll(xp, s, axis=0) * w[SHORT_CONV - 1 - s:SHORT_CONV - s, :]
        halo_ref[idx] = cur[tb - CONV_HALO:, :]
        return _silu(y[CONV_HALO:, :])

    qc = conv(q_ref, 0)
    kc = conv(k_ref, 1)
    vc = conv(v_ref, 2)

    gl = g_ref[...]
    beta_all = _sigmoid(gl)
    xg = gl + dtb_ref[...]
    softplus = jnp.maximum(xg, 0.0) + jnp.log(1.0 + jnp.exp(-jnp.abs(xg)))
    g_all = -jnp.exp(alog_ref[...]) * softplus

    ii = lax.broadcasted_iota(jnp.int32, (c, c), 0)
    jj = lax.broadcasted_iota(jnp.int32, (c, c), 1)
    incl = ii >= jj
    strict = ii > jj
    tri = jnp.where(incl, 1.0, 0.0).astype(F32)
    eye = jnp.where(ii == jj, 1.0, 0.0).astype(F32)
    gn = gn_ref[...]

    chunks = range(tb // c)
    heads = range(n_heads)
    pairs = [(ch, h) for ch in chunks for h in heads]

    gcs, gts = [], []
    for ch in chunks:
        gc_all = jnp.dot(tri, g_all[ch * c:(ch + 1) * c, :], precision=lax.Precision.HIGHEST,
                         preferred_element_type=F32)
        gcs.append(gc_all)
        gts.append(gc_all.T)

    ones16 = jnp.ones((hd, hd), BF16)

    def sumsq(x):
        return _dot((x * x).astype(BF16), ones16)

    qn, kn = {}, {}
    for h in heads:
        cs = slice(h * hd, (h + 1) * hd)
        qn[h] = qc[:, cs] * lax.rsqrt(sumsq(qc[:, cs]) + EPS) * (hd ** -0.5)
        kn[h] = kc[:, cs] * lax.rsqrt(sumsq(kc[:, cs]) + EPS)

    qs, ks, kbs, vbs, gcols, grows, lmats, amats = {}, {}, {}, {}, {}, {}, {}, {}
    for ch, h in pairs:
        rs = slice(ch * c, (ch + 1) * c)
        cs = slice(h * hd, (h + 1) * hd)
        q = qn[h][rs, :]
        k = kn[h][rs, :]
        gcol = gcs[ch][:, n_heads + h:n_heads + h + 1]
        grow = gts[ch][n_heads + h:n_heads + h + 1, :]
        bcol = beta_all[rs, h:h + 1]
        decay = jnp.exp(jnp.where(incl, gcol - grow, -jnp.inf))
        kb = k * bcol
        k16 = k.astype(BF16)
        lmats[ch, h] = jnp.where(strict, _dot_nt(kb.astype(BF16), k16) * decay, 0.0)
        amats[ch, h] = jnp.where(incl, _dot_nt(q.astype(BF16), k16) * decay, 0.0).astype(BF16)
        qs[ch, h], ks[ch, h], kbs[ch, h], vbs[ch, h] = q, k, kb, vc[rs, cs] * bcol
        gcols[ch, h], grows[ch, h] = gcol, grow

    tinvs = {p: eye - lmats[p] * lvl_ref[0] for p in pairs}
    for lvl in range(1, lvl_ref.shape[0]):
        t16s = {p: tinvs[p].astype(BF16) for p in pairs}
        ys = {p: _dot((lmats[p] * lvl_ref[lvl]).astype(BF16), t16s[p]).astype(BF16) for p in pairs}
        tinvs = {p: tinvs[p] - _dot(t16s[p], ys[p]) for p in pairs}

    us, ws = {}, {}
    for p in pairs:
        t16 = tinvs[p].astype(BF16)
        us[p] = _dot(t16, vbs[p].astype(BF16))
        ws[p] = _dot(t16, (kbs[p] * jnp.exp(gcols[p])).astype(BF16)).astype(BF16)

    for ch in chunks:
        rs = slice(ch * c, (ch + 1) * c)
        s16 = {h: s_ref[h].astype(BF16) for h in heads}
        vn16 = {h: (us[ch, h] - _dot(ws[ch, h], s16[h])).astype(BF16) for h in heads}
        for h in heads:
            p = (ch, h)
            cs = slice(h * hd, (h + 1) * hd)
            glast = grows[p][:, c - 1:c]
            kdec = ks[p] * jnp.exp(glast - gcols[p])
            s_ref[h] = s_ref[h] * jnp.exp(glast) + _dot_tn(kdec.astype(BF16), vn16[h])
            o = _dot((qs[p] * jnp.exp(gcols[p])).astype(BF16), s16[h]) + _dot(amats[p], vn16[h])
            ms = sumsq(o) * (1.0 / hd)
            o_ref[rs, cs] = (o * lax.rsqrt(ms + EPS) * gn * _silu(z_ref[rs, cs])).astype(BF16)


def _level_masks(c):
    i = lax.broadcasted_iota(jnp.int32, (c, c), 0)
    j = lax.broadcasted_iota(jnp.int32, (c, c), 1)
    levels = []
    for l in range(c.bit_length() - 1):
        same_big = (i >> (l + 1)) == (j >> (l + 1))
        diff_small = (i >> l) != (j >> l)
        levels.append(jnp.where(same_big & diff_small & (i > j), 1.0, 0.0))
    return jnp.stack(levels).astype(F32)


def _gdn(proj_a, gates, conv_w, alog_pad, dtb_pad, gdn_norm, *, n_heads):
    t = proj_a.shape[0]
    lvl = _level_masks(GDN_CHUNK)
    width = n_heads * HEAD_DIM
    tb = GDN_BLOCK
    row = lambda i: (0, 0)
    return pl.pallas_call(
        functools.partial(_gdn_kernel, n_heads=n_heads),
        grid=(t // tb,),
        in_specs=[
            pl.BlockSpec((tb, width), lambda i: (i, 0)),
            pl.BlockSpec((tb, width), lambda i: (i, 1)),
            pl.BlockSpec((tb, width), lambda i: (i, 2)),
            pl.BlockSpec((tb, width), lambda i: (i, 3)),
            pl.BlockSpec((tb, LANES), lambda i: (i, 0)),
            pl.BlockSpec((SHORT_CONV, 3 * width), row),
            pl.BlockSpec((1, LANES), row),
            pl.BlockSpec((1, LANES), row),
            pl.BlockSpec((1, HEAD_DIM), row),
            pl.BlockSpec(lvl.shape, lambda i: (0, 0, 0)),
        ],
        out_specs=pl.BlockSpec((tb, width), lambda i: (i, 0)),
        out_shape=jax.ShapeDtypeStruct((t, width), BF16),
        scratch_shapes=[
            pltpu.VMEM((n_heads, HEAD_DIM, HEAD_DIM), F32),
            pltpu.VMEM((3, CONV_HALO, width), F32),
        ],
        compiler_params=_params(("arbitrary",)),
        name="gdn",
    )(proj_a, proj_a, proj_a, proj_a, gates, conv_w, alog_pad, dtb_pad, gdn_norm, lvl)


def _sb_kernel(zb_ref, q_ref, k_ref, v_ref, u_ref, o_ref, acc_ref, run_ref, *, group):
    i = pl.program_id(1)
    b = SB_BLOCK
    zb = zb_ref[0]
    umat = u_ref[...]
    row = lax.broadcasted_iota(jnp.int32, (b, b), 0)
    col = lax.broadcasted_iota(jnp.int32, (b, b), 1)

    def sweep(it, diagonal):
        zs, masks, his, los = [], [], [], []
        for g in range(group):
            j = i * group + g - it
            start = pl.multiple_of(jnp.maximum(j, 0) * b, b)
            z = _dot_nt(q_ref[g * b:(g + 1) * b, :], k_ref[pl.ds(start, b), :])
            mask = (col < row) if diagonal else jnp.broadcast_to(j >= 0, (b, b))
            ls = jnp.where(mask, -(jnp.maximum(z, 0.0) + jnp.log(1.0 + jnp.exp(-jnp.abs(z)))), 0.0)
            hi, lo = _split16(ls)
            zs.append(z), masks.append(mask), his.append(hi), los.append(lo)
        cum = _dot(jnp.concatenate(his + los, axis=0), umat)
        maxes = []
        for g in range(group):
            j = i * group + g - it
            start = pl.multiple_of(jnp.maximum(j, 0) * b, b)
            c = cum[g * b:(g + 1) * b, :] + cum[(group + g) * b:(group + g + 1) * b, :]
            run = run_ref[g]
            a = jnp.where(masks[g], jnp.exp(zs[g] + run + c[:, :b]), 0.0)
            acc_ref[g] += _dot(a.astype(BF16), v_ref[pl.ds(start, b), :])
            run = run + c[:, b:]
            run_ref[g] = run
            maxes.append(jnp.max(run))
        return maxes

    acc_ref[...] = jnp.zeros_like(acc_ref)
    run_ref[...] = jnp.zeros_like(run_ref)
    first = sweep(0, True)

    def cond(carry):
        it, ms = carry[0], carry[1:]
        go = None
        for g in range(group):
            act = jnp.logical_and(i * group + g - it >= 0, ms[g] + zb > F32_EXP_ZERO_BELOW)
            go = act if go is None else jnp.logical_or(go, act)
        return go

    def body(carry):
        it = carry[0]
        return (it + 1, *sweep(it, False))

    lax.while_loop(cond, body, (jnp.int32(1), *first))
    for g in range(group):
        o_ref[g * b:(g + 1) * b, :] = acc_ref[g].astype(BF16)


def _stick_breaking(proj_b, zbound, *, n_heads):
    t = proj_b.shape[0]
    b = SB_BLOCK
    group = SB_GROUP
    rows = group * b
    assert t % rows == 0
    r = lax.broadcasted_iota(jnp.int32, (b, 2 * b), 0)
    c = lax.broadcasted_iota(jnp.int32, (b, 2 * b), 1)
    umat = jnp.where((r >= c) | (c >= b), 1.0, 0.0).astype(BF16)
    return pl.pallas_call(
        functools.partial(_sb_kernel, group=group),
        grid=(n_heads, t // rows),
        in_specs=[
            pl.BlockSpec(memory_space=pltpu.SMEM),
            pl.BlockSpec((rows, HEAD_DIM), lambda h, i: (i, h)),
            pl.BlockSpec((t, HEAD_DIM), lambda h, i: (0, n_heads + h)),
            pl.BlockSpec((t, HEAD_DIM), lambda h, i: (0, 2 * n_heads + h)),
            pl.BlockSpec(umat.shape, lambda h, i: (0, 0)),
        ],
        out_specs=pl.BlockSpec((rows, HEAD_DIM), lambda h, i: (i, h)),
        out_shape=jax.ShapeDtypeStruct((t, n_heads * HEAD_DIM), BF16),
        scratch_shapes=[pltpu.VMEM((group, b, HEAD_DIM), F32), pltpu.VMEM((group, b, b), F32)],
        compiler_params=_params(("arbitrary", "arbitrary")),
        name="stick_breaking",
    )(zbound, proj_b, proj_b, proj_b, umat)


def _outproj_kernel(x_ref, oa_ref, ob_ref, wa_ref, wb_ref, gate_ref, o_ref):
    y = _dot(oa_ref[...], wa_ref[...]) + _dot(ob_ref[...], wb_ref[...])
    o_ref[...] = x_ref[...] + gate_ref[...] * y


def _outproj(x, o_a, o_b, w_out16, gate, *, tm=512):
    t, d = x.shape
    da = o_a.shape[1]
    db = o_b.shape[1]
    return pl.pallas_call(
        _outproj_kernel,
        grid=(t // tm,),
        in_specs=[
            pl.BlockSpec((tm, d), lambda m: (m, 0)),
            pl.BlockSpec((tm, da), lambda m: (m, 0)),
            pl.BlockSpec((tm, db), lambda m: (m, 0)),
            pl.BlockSpec((da, d), lambda m: (0, 0)),
            pl.BlockSpec((db, d), lambda m: (1, 0)),
            pl.BlockSpec((1, d), lambda m: (0, 0)),
        ],
        out_specs=pl.BlockSpec((tm, d), lambda m: (m, 0)),
        out_shape=jax.ShapeDtypeStruct((t, d), F32),
        compiler_params=_params(("arbitrary",)),
        name="outproj",
    )(x, o_a, o_b, w_out16, w_out16, gate)


def _ffn_kernel(x_ref, halo_ref, g_ref, sc_ref, sh_ref, gate_ref, wg_ref, wv_ref, cw_ref, wd_ref,
                o_ref, h_ref, acc_ref, *, rows):
    m = pl.program_id(0)
    f = pl.program_id(1)
    nf = pl.num_programs(1)
    tm = x_ref.shape[0]

    @pl.when(f == 0)
    def _():
        gain = g_ref[...] * (1.0 + sc_ref[...])
        shift = sh_ref[...]

        def norm(xf):
            ms = jnp.mean(xf * xf, axis=-1, keepdims=True)
            return (xf * lax.rsqrt(ms + EPS) * gain + shift).astype(BF16)

        keep = jnp.where(m > 0, 1.0, 0.0).astype(F32)
        h_ref[0:FFN_HALO, :] = (norm(halo_ref[...]).astype(F32) * keep).astype(BF16)

        def body(r, carry):
            src = pl.ds(pl.multiple_of(r * rows, rows), rows)
            dst = pl.ds(pl.multiple_of(FFN_HALO + r * rows, FFN_HALO), rows)
            h_ref[dst, :] = norm(x_ref[src, :])
            return carry

        lax.fori_loop(0, tm // rows, body, 0)
        acc_ref[...] = jnp.zeros_like(acc_ref)

    ug = _dot(h_ref[...], wg_ref[...])
    uv = _dot(h_ref[FFN_HALO:, :], wv_ref[...])
    cw = cw_ref[...]
    y = ug * cw[FFN_CONV - 1:FFN_CONV, :]
    for s in range(1, FFN_CONV):
        y = y + pltpu.roll(ug, s, axis=0) * cw[FFN_CONV - 1 - s:FFN_CONV - s, :]
    act = (_silu(y[FFN_HALO:, :]) * uv).astype(BF16)
    acc_ref[...] += _dot(act, wd_ref[...])

    @pl.when(f == nf - 1)
    def _():
        o_ref[...] = x_ref[...] + gate_ref[...] * acc_ref[...]


def _ffn(x1, gain, scale, shift, gate, w_up16, conv_w, w_down16, *, tm=1024, tf=512):
    t, d = x1.shape
    d_ff = w_down16.shape[0]
    nf = d_ff // tf
    row = lambda m, f: (0, 0)
    halo_blocks = tm // FFN_HALO
    return pl.pallas_call(
        functools.partial(_ffn_kernel, rows=128),
        grid=(t // tm, nf),
        in_specs=[
            pl.BlockSpec((tm, d), lambda m, f: (m, 0), pipeline_mode=pl.Buffered(1)),
            pl.BlockSpec((FFN_HALO, d), lambda m, f: (jnp.maximum(m * halo_blocks - 1, 0), 0)),
            pl.BlockSpec((1, d), row),
            pl.BlockSpec((1, d), row),
            pl.BlockSpec((1, d), row),
            pl.BlockSpec((1, d), row),
            pl.BlockSpec((d, tf), lambda m, f: (0, f)),
            pl.BlockSpec((d, tf), lambda m, f: (0, nf + f)),
            pl.BlockSpec((FFN_CONV, tf), lambda m, f: (0, f)),
            pl.BlockSpec((tf, d), lambda m, f: (f, 0)),
        ],
        out_specs=pl.BlockSpec((tm, d), lambda m, f: (m, 0), pipeline_mode=pl.Buffered(1)),
        out_shape=jax.ShapeDtypeStruct((t, d), F32),
        scratch_shapes=[pltpu.VMEM((tm + FFN_HALO, d), BF16), pltpu.VMEM((tm, d), F32)],
        compiler_params=_params(("arbitrary", "arbitrary")),
        name="convffn",
    )(x1, x1, gain, scale, shift, gate, w_up16, w_up16, conv_w, w_down16)


def kernel(x, c, w_ada, b_ada, norm1, w_in, conv_qkv, a_log, dt_bias, gdn_norm, sb_q_norm,
           sb_k_norm, w_out, norm2, w_up, conv_ffn, w_down):
    batch, t, d = x.shape
    assert batch == 1
    depth = w_ada.shape[0]
    n_heads = a_log.shape[1]
    d_gdn = n_heads * HEAD_DIM
    d_sb = (w_in.shape[2] - 4 * d_gdn - 2 * n_heads) // 3
    assert d_sb == d_gdn and 2 * n_heads <= LANES
    o_gate = 4 * d_gdn
    o_sb = o_gate + 2 * n_heads

    mod = _modulation(c, w_ada, b_ada)
    xs = x.reshape(t, d)
    for l in range(depth):
        shift1, scale1, gate1, shift2, scale2, gate2 = [mod[l, :, j * d:(j + 1) * d] for j in range(6)]
        w_main = jnp.concatenate([w_in[l, :, :o_gate], w_in[l, :, o_sb:]], axis=1).astype(BF16)
        w_gate = jnp.pad(w_in[l, :, o_gate:o_sb], ((0, 0), (0, LANES - 2 * n_heads))).astype(BF16)
        proj_a, proj_b, gates = _inproj(
            xs, norm1[l].reshape(1, d), scale1, shift1, w_main, w_gate,
            sb_q_norm[l].reshape(1, HEAD_DIM), sb_k_norm[l].reshape(1, HEAD_DIM))

        pad = (n_heads, LANES - 2 * n_heads)
        alog_pad = jnp.pad(a_log[l], pad).reshape(1, LANES)
        dtb_pad = jnp.pad(dt_bias[l], pad).reshape(1, LANES)
        o_a = _gdn(proj_a, gates, conv_qkv[l], alog_pad, dtb_pad,
                   gdn_norm[l].reshape(1, HEAD_DIM), n_heads=n_heads)

        zbound = (1.02 * HEAD_DIM ** 0.5 * jnp.max(jnp.abs(sb_q_norm[l]))
                  * jnp.max(jnp.abs(sb_k_norm[l]))).reshape(1)
        o_b = _stick_breaking(proj_b, zbound, n_heads=n_heads)

        x1 = _outproj(xs, o_a, o_b, w_out[l].astype(BF16), gate1)
        xs = _ffn(x1, norm2[l].reshape(1, d), scale2, shift2, gate2,
                  w_up[l].astype(BF16), conv_ffn[l], w_down[l].astype(BF16))
    return xs.reshape(batch, t, d)
```
